```python
import jax
import jax.numpy as jnp
from jax import lax
import numpy as np

D_MODEL = 4096
BATCH = 2
SEQ = 8192
DEPTH = 2

HEAD_DIM = 128
ROPE_THETA = 10000.0
NORM_EPS = 1e-6
NEG_INF = -1e30
FORCE_SCORE = 1e9

NSA_Q_HEADS = 16
NSA_KV_HEADS = 4
NSA_HPG = NSA_Q_HEADS // NSA_KV_HEADS
NSA_CMP_LEN = 32
NSA_CMP_STRIDE = 16
NSA_CMP_HIDDEN = HEAD_DIM
NSA_SEL_BLOCK = 64
NSA_TOP_N = 16
NSA_WINDOW = 512
NSA_Q_CHUNK = 64

MOBA_HEADS = 16
MOBA_BLOCK = 256
MOBA_TOPK = 3
MOBA_Q_CHUNK = 32

NSA_Q_W = NSA_Q_HEADS * HEAD_DIM
NSA_KV_W = NSA_KV_HEADS * HEAD_DIM
NSA_GATE_W = NSA_Q_HEADS * 3
MOBA_W = MOBA_HEADS * HEAD_DIM
IN_SPLITS = (NSA_Q_W, 6 * NSA_KV_W, NSA_GATE_W, 3 * MOBA_W, 2 * D_MODEL)
IN_COLS = sum(IN_SPLITS)
IN_OFFSETS = tuple(int(v) for v in np.cumsum(IN_SPLITS)[:-1])

MOE_GROUPS = 4
MOE_EXPERTS_PER_GROUP = 8
MOE_EXPERTS = MOE_GROUPS * MOE_EXPERTS_PER_GROUP
MOE_TOPK = 2
MOE_FF = 768
MOE_ROW_BLOCK = 128

PLE_DIM = 256

kernel_name = "hybrid_nsa_moba_hmoe_block"


def rms_norm(x, g):
    xf = x.astype(jnp.float32)
    y = xf * lax.rsqrt(jnp.mean(xf * xf, axis=-1, keepdims=True) + NORM_EPS)
    return (y * g.astype(jnp.float32)).astype(x.dtype)


def rope(x, pos):
    half = HEAD_DIM // 2
    inv = jnp.power(ROPE_THETA, -jnp.arange(half, dtype=jnp.float32) / half)
    ang = pos.astype(jnp.float32)[:, None] * inv
    shape = (1, pos.shape[0]) + (1,) * (x.ndim - 3) + (half,)
    c = jnp.cos(ang).reshape(shape)
    s = jnp.sin(ang).reshape(shape)
    xf = x.astype(jnp.float32)
    x1, x2 = xf[..., :half], xf[..., half:]
    return jnp.concatenate([x1 * c - x2 * s, x2 * c + x1 * s], axis=-1).astype(x.dtype)


def masked_softmax(s, mask):
    return jax.nn.softmax(jnp.where(mask, s.astype(jnp.float32), NEG_INF), axis=-1)


def nsa_compress(xr, pe, w1, w2):
    S = xr.shape[1]
    nc = (S - NSA_CMP_LEN) // NSA_CMP_STRIDE + 1
    idx = np.arange(nc)[:, None] * NSA_CMP_STRIDE + np.arange(NSA_CMP_LEN)[None, :]
    blk = xr[:, idx] + pe[None, None, :, None, :]
    hid = jax.nn.gelu(jnp.einsum("bnlgd,lde->bnge", blk, w1))
    return jnp.einsum("bnge,ed->bngd", hid, w2)


def cmp_to_sel_overlap(nc, ns):
    cs = np.arange(nc)[:, None] * NSA_CMP_STRIDE
    ss = np.arange(ns)[None, :] * NSA_SEL_BLOCK
    ov = np.clip(np.minimum(cs + NSA_CMP_LEN, ss + NSA_SEL_BLOCK) - np.maximum(cs, ss), 0, None)
    return (ov / NSA_CMP_LEN).astype(np.float32)


def nsa_attention(q, kc, vc, ks, vs, kw, vw, gates):
    B, S = q.shape[0], q.shape[1]
    G, dk, QC, WIN, SEL = NSA_KV_HEADS, HEAD_DIM, NSA_Q_CHUNK, NSA_WINDOW, NSA_SEL_BLOCK
    nc = kc.shape[1]
    ns = S // SEL
    top_n = min(NSA_TOP_N, ns)
    scale = HEAD_DIM ** -0.5
    cmp_end = jnp.arange(nc) * NSA_CMP_STRIDE + NSA_CMP_LEN - 1
    overlap = jnp.asarray(cmp_to_sel_overlap(nc, ns))
    ks_blk = ks.reshape(B, ns, SEL, G, dk).transpose(0, 3, 1, 2, 4)
    vs_blk = vs.reshape(B, ns, SEL, G, dk).transpose(0, 3, 1, 2, 4)
    pad = ((0, 0), (WIN, 0), (0, 0), (0, 0))
    kw_pad = jnp.pad(kw, pad)
    vw_pad = jnp.pad(vw, pad)
    bi = jnp.arange(B)[:, None, None, None]
    gi = jnp.arange(G)[None, :, None, None]
    sel_ids = jnp.arange(ns)[None, :]
    sel_off = jnp.arange(SEL)
    win_off = jnp.arange(QC + WIN)

    def chunk(c):
        t0 = c * QC
        pos = t0 + jnp.arange(QC)
        qc = lax.dynamic_slice_in_dim(q, t0, QC, axis=1)
        valid_c = cmp_end[None, :] <= pos[:, None]
        s_c = jnp.einsum("bqghd,bngd->bghqn", qc, kc) * scale
        p_c = masked_softmax(s_c, valid_c) * jnp.any(valid_c, axis=-1)[:, None].astype(jnp.float32)
        o_c = jnp.einsum("bghqn,bngd->bqghd", p_c.astype(vc.dtype), vc)
        imp = jnp.einsum("bghqn,nj->bgqj", p_c, overlap)
        cur = (pos // SEL)[:, None]
        forced = (sel_ids == 0) | (sel_ids == cur) | (sel_ids == cur - 1)
        imp = jnp.where(sel_ids <= cur, jnp.where(forced, FORCE_SCORE, imp), NEG_INF)
        _, idx = lax.top_k(imp, top_n)
        kg = ks_blk[bi, gi, idx].reshape(B, G, QC, top_n * SEL, dk)
        vg = vs_blk[bi, gi, idx].reshape(B, G, QC, top_n * SEL, dk)
        kpos = (idx[..., None] * SEL + sel_off).reshape(B, G, QC, top_n * SEL)
        mask_s = (kpos <= pos[:, None])[:, :, None]
        s_s = jnp.einsum("bqghd,bgqkd->bghqk", qc, kg) * scale
        p_s = masked_softmax(s_s, mask_s)
        o_s = jnp.einsum("bghqk,bgqkd->bqghd", p_s.astype(vg.dtype), vg)
        kwc = lax.dynamic_slice_in_dim(kw_pad, t0, QC + WIN, axis=1)
        vwc = lax.dynamic_slice_in_dim(vw_pad, t0, QC + WIN, axis=1)
        kp = t0 - WIN + win_off
        dist = pos[:, None] - kp[None, :]
        mask_w = (dist >= 0) & (dist < WIN) & (kp[None, :] >= 0)
        s_w = jnp.einsum("bqghd,bkgd->bghqk", qc, kwc) * scale
        p_w = masked_softmax(s_w, mask_w)
        o_w = jnp.einsum("bghqk,bkgd->bqghd", p_w.astype(vwc.dtype), vwc)
        g = lax.dynamic_slice_in_dim(gates, t0, QC, axis=1)
        return g[..., 0:1] * o_c + g[..., 1:2] * o_s + g[..., 2:3] * o_w

    out = lax.map(chunk, jnp.arange(S // QC))
    return out.transpose(1, 0, 2, 3, 4, 5).reshape(B, S, NSA_Q_W)


def moba_attention(q, k, v):
    B, S, H, dk = q.shape
    BLK, QC = MOBA_BLOCK, MOBA_Q_CHUNK
    nqb = -(-S // BLK)
    nf = S // BLK
    k_sel = min(MOBA_TOPK, nqb - 1)
    scale = HEAD_DIM ** -0.5
    pad = ((0, 0), (0, nqb * BLK - S), (0, 0), (0, 0))
    k_pad = jnp.pad(k, pad)
    v_pad = jnp.pad(v, pad)
    if k_sel > 0:
        k_full = k[:, :nf * BLK].reshape(B, nf, BLK, H, dk)
        k_mean = jnp.mean(k_full.astype(jnp.float32), axis=2).astype(k.dtype)
        kb = k_full.transpose(0, 3, 1, 2, 4)
        vb = v[:, :nf * BLK].reshape(B, nf, BLK, H, dk).transpose(0, 3, 1, 2, 4)
        bi = jnp.arange(B)[:, None, None, None]
        hi = jnp.arange(H)[None, :, None, None]

    def chunk(c):
        t0 = c * QC
        pos = t0 + jnp.arange(QC)
        qc = lax.dynamic_slice_in_dim(q, t0, QC, axis=1)
        qb = t0 // BLK
        ko = lax.dynamic_slice_in_dim(k_pad, qb * BLK, BLK, axis=1)
        vo = lax.dynamic_slice_in_dim(v_pad, qb * BLK, BLK, axis=1)
        kpos = qb * BLK + jnp.arange(BLK)
        s_o = jnp.where(kpos[None, :] <= pos[:, None],
                        (jnp.einsum("bqhd,bkhd->bhqk", qc, ko) * scale).astype(jnp.float32), NEG_INF)
        if k_sel == 0:
            p = jax.nn.softmax(s_o, axis=-1)
            return jnp.einsum("bhqk,bkhd->bqhd", p.astype(vo.dtype), vo)
        gate = jnp.einsum("bqhd,bnhd->bhqn", qc, k_mean).astype(jnp.float32)
        gate = jnp.where(jnp.arange(nf) < qb, gate, NEG_INF)
        _, idx = lax.top_k(gate, k_sel)
        slot_ok = jnp.repeat(idx < qb, BLK, axis=-1)
        kg = kb[bi, hi, idx].reshape(B, H, QC, k_sel * BLK, dk)
        vg = vb[bi, hi, idx].reshape(B, H, QC, k_sel * BLK, dk)
        s_g = jnp.where(slot_ok, (jnp.einsum("bqhd,bhqkd->bhqk", qc, kg) * scale).astype(jnp.float32), NEG_INF)
        p = jax.nn.softmax(jnp.concatenate([s_g, s_o], axis=-1), axis=-1)
        p_g = p[..., :k_sel * BLK].astype(vg.dtype)
        p_o = p[..., k_sel * BLK:].astype(vo.dtype)
        return jnp.einsum("bhqk,bhqkd->bqhd", p_g, vg) + jnp.einsum("bhqk,bkhd->bqhd", p_o, vo)

    out = lax.map(chunk, jnp.arange(S // QC))
    return out.transpose(1, 0, 2, 3, 4).reshape(B, S, MOBA_W)


def hier_moe(x2, w_rg, w_re, w_gate, w_up, w_down):
    T, D = x2.shape
    pg = jax.nn.softmax((x2 @ w_rg).astype(jnp.float32), axis=-1)
    pg_top, grp = lax.top_k(pg, 1)
    le = (x2 @ w_re).astype(jnp.float32).reshape(T, MOE_GROUPS, MOE_EXPERTS_PER_GROUP)
    le = le[jnp.arange(T), grp[:, 0]]
    le_top, j = lax.top_k(le, MOE_TOPK)
    wts = jax.nn.softmax(le_top, axis=-1) * pg_top
    eid = grp * MOE_EXPERTS_PER_GROUP + j
    ROW = MOE_ROW_BLOCK
    n_assign = T * MOE_TOPK
    n_blocks = -(-n_assign // ROW) + MOE_EXPERTS
    e_flat = eid.reshape(-1)
    tok = jnp.repeat(jnp.arange(T, dtype=jnp.int32), MOE_TOPK)
    w_flat = wts.reshape(-1)
    order = jnp.argsort(e_flat)
    e_s, tok_s, w_s = e_flat[order], tok[order], w_flat[order]
    counts = jnp.bincount(e_flat, length=MOE_EXPERTS).astype(jnp.int32)
    starts = jnp.cumsum(counts) - counts
    pcounts = (counts + ROW - 1) // ROW * ROW
    pends = jnp.cumsum(pcounts)
    pstarts = pends - pcounts
    dest = pstarts[e_s] + (jnp.arange(n_assign, dtype=jnp.int32) - starts[e_s])
    buf_tok = jnp.full((n_blocks * ROW,), T, dtype=jnp.int32).at[dest].set(tok_s)
    buf_w = jnp.zeros((n_blocks * ROW,), jnp.float32).at[dest].set(w_s)
    blk_e = jnp.minimum(jnp.searchsorted(pends, jnp.arange(n_blocks) * ROW, side="right"), MOE_EXPERTS - 1)
    x_pad = jnp.concatenate([x2, jnp.zeros((1, D), x2.dtype)], axis=0)

    def expert_block(args):
        tok_b, e = args
        xb = x_pad[tok_b]
        hb = jax.nn.silu(xb @ w_gate[e]) * (xb @ w_up[e])
        return hb @ w_down[e]

    out = lax.map(expert_block, (buf_tok.reshape(n_blocks, ROW), blk_e)).reshape(n_blocks * ROW, D)
    y = jnp.zeros((T + 1, D), x2.dtype).at[buf_tok].add(out * buf_w[:, None].astype(x2.dtype))
    return y[:T]


def _normal(k, shape, scale):
    return jax.random.normal(k, shape, jnp.float32) * scale


def _gain(k, shape):
    return 1.0 + 0.02 * jax.random.normal(k, shape, jnp.float32)


def setup_inputs(seed: int = 0) -> dict:
    key = jax.random.key(seed)
    ks = jax.random.split(key, 20)
    L, D = DEPTH, D_MODEL
    return {
        "x": _normal(ks[0], (BATCH, SEQ, D), 1.0),
        "p": _normal(ks[1], (DEPTH, BATCH, SEQ, PLE_DIM), 1.0),
        "attn_norm": _gain(ks[2], (L, D)),
        "w_in": _normal(ks[3], (L, D, IN_COLS), D ** -0.5),
        "nsa_cmp_pe": _normal(ks[4], (L, 2, NSA_CMP_LEN, HEAD_DIM), 0.02),
        "nsa_cmp_w1": _normal(ks[5], (L, 2, NSA_CMP_LEN, HEAD_DIM, NSA_CMP_HIDDEN), (NSA_CMP_LEN * HEAD_DIM) ** -0.5),
        "nsa_cmp_w2": _normal(ks[6], (L, 2, NSA_CMP_HIDDEN, HEAD_DIM), NSA_CMP_HIDDEN ** -0.5),
        "w_up_nsa": _normal(ks[7], (L, NSA_Q_W, D), NSA_Q_W ** -0.5),
        "w_up_moba": _normal(ks[8], (L, MOBA_W, D), MOBA_W ** -0.5),
        "w_out": _normal(ks[9], (L, D, D), D ** -0.5),
        "ffn_norm": _gain(ks[10], (L, D)),
        "router_group": _normal(ks[11], (L, D, MOE_GROUPS), D ** -0.5),
        "router_expert": _normal(ks[12], (L, D, MOE_EXPERTS), D ** -0.5),
        "expert_w_gate": _normal(ks[13], (L, MOE_EXPERTS, D, MOE_FF), D ** -0.5),
        "expert_w_up": _normal(ks[14], (L, MOE_EXPERTS, D, MOE_FF), D ** -0.5),
        "expert_w_down": _normal(ks[15], (L, MOE_EXPERTS, MOE_FF, D), MOE_FF ** -0.5),
        "ple_norm": _gain(ks[16], (L, D)),
        "ple_gate": _normal(ks[17], (L, D, D), D ** -0.5),
        "ple_proj": _normal(ks[18], (L, PLE_DIM, D), PLE_DIM ** -0.5),
        "final_norm": _gain(ks[19], (D,)),
    }


def reference(x, p, attn_norm, w_in, nsa_cmp_pe, nsa_cmp_w1, nsa_cmp_w2, w_up_nsa, w_up_moba, w_out,
              ffn_norm, router_group, router_expert, expert_w_gate, expert_w_up, expert_w_down,
              ple_norm, ple_gate, ple_proj, final_norm):
    B, S, D = x.shape
    G, HPG, dk = NSA_KV_HEADS, NSA_HPG, HEAD_DIM
    pos = jnp.arange(S)
    nc = (S - NSA_CMP_LEN) // NSA_CMP_STRIDE + 1
    cmp_pos = jnp.arange(nc) * NSA_CMP_STRIDE + NSA_CMP_LEN - 1
    h = x
    for i in range(DEPTH):
        a = rms_norm(h, attn_norm[i])
        z = a @ w_in[i]
        q_a, kv_a, g_a, qkv_b, merge = jnp.split(z, IN_OFFSETS, axis=-1)
        q_a = rope(q_a.reshape(B, S, G, HPG, dk), pos)
        kc, vc, ksl, vsl, kwn, vwn = [t.reshape(B, S, G, dk) for t in jnp.split(kv_a, 6, axis=-1)]
        kc = rope(nsa_compress(kc, nsa_cmp_pe[i, 0], nsa_cmp_w1[i, 0], nsa_cmp_w2[i, 0]), cmp_pos)
        vc = nsa_compress(vc, nsa_cmp_pe[i, 1], nsa_cmp_w1[i, 1], nsa_cmp_w2[i, 1])
        gates = jax.nn.sigmoid(g_a.reshape(B, S, G, HPG, 3))
        o_a = nsa_attention(q_a, kc, vc, rope(ksl, pos), vsl, rope(kwn, pos), vwn, gates)
        q_b, k_b, v_b = [t.reshape(B, S, MOBA_HEADS, dk) for t in jnp.split(qkv_b, 3, axis=-1)]
        o_b = moba_attention(rope(q_b, pos), rope(k_b, pos), v_b)
        m_a, m_b = jnp.split(merge, 2, axis=-1)
        mixed = jax.nn.sigmoid(m_a) * (o_a @ w_up_nsa[i]) + jax.nn.sigmoid(m_b) * (o_b @ w_up_moba[i])
        h = h + mixed @ w_out[i]
        f = rms_norm(h, ffn_norm[i]).reshape(B * S, D)
        h = h + hier_moe(f, router_group[i], router_expert[i], expert_w_gate[i],
                         expert_w_up[i], expert_w_down[i]).reshape(B, S, D)
        pg = jax.nn.sigmoid(rms_norm(h, ple_norm[i]) @ ple_gate[i])
        h = h + pg * (p[i] @ ple_proj[i])
    return rms_norm(h, final_norm)
```

```python
import functools

import numpy as np
import jax
import jax.numpy as jnp
from jax import lax
from jax.experimental import pallas as pl
from jax.experimental.pallas import tpu as pltpu

HEAD_DIM = 128
ROPE_THETA = 10000.0
NORM_EPS = 1e-6
NEG_INF = -1e30
FORCE_SCORE = 1e9
SCALE = HEAD_DIM ** -0.5

NSA_Q_HEADS = 16
NSA_KV_HEADS = 4
NSA_HPG = NSA_Q_HEADS // NSA_KV_HEADS
NSA_CMP_LEN = 32
NSA_CMP_STRIDE = 16
NSA_SEL_BLOCK = 64
NSA_TOP_N = 16
NSA_WINDOW = 512
NSA_Q_W = NSA_Q_HEADS * HEAD_DIM
NSA_KV_W = NSA_KV_HEADS * HEAD_DIM
NSA_GATE_W = NSA_Q_HEADS * 3

MOBA_HEADS = 16
MOBA_BLOCK = 256
MOBA_TOPK = 3
MOBA_W = MOBA_HEADS * HEAD_DIM

MOE_GROUPS = 4
MOE_EXPERTS_PER_GROUP = 8
MOE_EXPERTS = MOE_GROUPS * MOE_EXPERTS_PER_GROUP
MOE_TOPK = 2

LANES = 128
SEL_PAD = 128
GATE_PAD = 512
VMEM_LIMIT = 48 * 1024 * 1024

COL_QA = 0
COL_KC = NSA_Q_W
COL_VC = COL_KC + NSA_KV_W
COL_KS = COL_VC + NSA_KV_W
COL_VS = COL_KS + NSA_KV_W
COL_KW = COL_VS + NSA_KV_W
COL_VW = COL_KW + NSA_KV_W
COL_QB = COL_VW + NSA_KV_W
COL_KB = COL_QB + MOBA_W
COL_VB = COL_KB + MOBA_W
COL_MERGE = COL_VB + MOBA_W

_CD = jnp.bfloat16
_F32 = jnp.float32


def _dot(a, b):
    return jnp.dot(a, b, preferred_element_type=_F32)


def _dot_nt(a, b):
    return lax.dot_general(a, b, (((1,), (1,)), ((), ())), preferred_element_type=_F32)


def _split(x):
    hi = x.astype(_CD)
    lo = (x - hi.astype(_F32)).astype(_CD)
    return hi, lo


def _params(sem, vmem=VMEM_LIMIT):
    return pltpu.CompilerParams(dimension_semantics=sem, vmem_limit_bytes=vmem)


def _rmsnorm_body(x_ref, g_ref, o_ref):
    x = x_ref[...]
    y = x * lax.rsqrt(jnp.mean(x * x, axis=-1, keepdims=True) + NORM_EPS)
    o_ref[...] = (y * g_ref[...]).astype(o_ref.dtype)


def _rmsnorm(x, g, out_dtype, tm=256):
    T, D = x.shape
    return pl.pallas_call(
        _rmsnorm_body,
        grid=(T // tm,),
        in_specs=[pl.BlockSpec((tm, D), lambda i: (i, 0)),
                  pl.BlockSpec((1, D), lambda i: (0, 0))],
        out_specs=pl.BlockSpec((tm, D), lambda i: (i, 0)),
        out_shape=jax.ShapeDtypeStruct((T, D), out_dtype),
        compiler_params=_params(("parallel",)),
        name="rmsnorm",
    )(x, g.reshape(1, D))


def _mm_call(body, lhs, rhs, extras, n_out, out_dtype, tm, tn, name):
    M = lhs[0].shape[0]
    in_specs = [pl.BlockSpec((tm, a.shape[1]), lambda i, j: (i, 0)) for a in lhs]
    in_specs += [pl.BlockSpec((w.shape[0], tn), lambda i, j: (0, j)) for w in rhs]
    args = list(lhs) + list(rhs)
    for arr, off in extras:
        assert off % tn == 0
        in_specs.append(pl.BlockSpec((tm, tn), functools.partial(lambda i, j, o: (i, j + o), o=off // tn)))
        args.append(arr)
    return pl.pallas_call(
        body,
        grid=(M // tm, n_out // tn),
        in_specs=in_specs,
        out_specs=pl.BlockSpec((tm, tn), lambda i, j: (i, j)),
        out_shape=jax.ShapeDtypeStruct((M, n_out), out_dtype),
        compiler_params=_params(("parallel", "parallel")),
        name=name,
    )(*args)


def _mm_plain_body(a_ref, w_ref, o_ref):
    o_ref[...] = _dot(a_ref[...], w_ref[...]).astype(o_ref.dtype)


def _mm_gated_body(a1_ref, a2_ref, w1_ref, w2_ref, m1_ref, m2_ref, o_ref):
    y1 = _dot(a1_ref[...], w1_ref[...])
    y2 = _dot(a2_ref[...], w2_ref[...])
    o_ref[...] = (jax.nn.sigmoid(m1_ref[...]) * y1 + jax.nn.sigmoid(m2_ref[...]) * y2).astype(o_ref.dtype)


def _mm_resid_body(a_ref, w_ref, r_ref, o_ref):
    o_ref[...] = r_ref[...] + _dot(a_ref[...], w_ref[...])


def _mm_ple_body(n_ref, p_ref, wg_ref, wp_ref, r_ref, o_ref):
    gate = jax.nn.sigmoid(_dot(n_ref[...], wg_ref[...]))
    emb = _dot(p_ref[...].astype(_CD), wp_ref[...])
    o_ref[...] = r_ref[...] + gate * emb


def _rope_tables(pos):
    half = HEAD_DIM // 2
    inv = jnp.power(ROPE_THETA, -jnp.arange(half, dtype=_F32) / half)
    ang = pos.astype(_F32)[:, None] * inv
    c, s = jnp.cos(ang), jnp.sin(ang)
    return jnp.concatenate([c, c], axis=-1), jnp.concatenate([-s, s], axis=-1)


def _rope(x, c, s):
    return x * c + pltpu.roll(x, HEAD_DIM // 2, 1) * s


PREP_TN = 512
_ROPE_BLOCKS = ((COL_QA // PREP_TN, COL_KC // PREP_TN), (COL_KS // PREP_TN, COL_VS // PREP_TN),
                (COL_KW // PREP_TN, COL_VW // PREP_TN), (COL_QB // PREP_TN, COL_VB // PREP_TN))


def _prep_body(x_ref, cos_ref, sin_ref, o_ref):
    j = pl.program_id(1)
    is_rope = functools.reduce(jnp.logical_or, [(j >= a) & (j < b) for a, b in _ROPE_BLOCKS])

    @pl.when(is_rope)
    def _():
        c = cos_ref[...]
        s = sin_ref[...]
        for h in range(PREP_TN // HEAD_DIM):
            sl = slice(h * HEAD_DIM, (h + 1) * HEAD_DIM)
            o_ref[:, sl] = _rope(x_ref[:, sl], c, s).astype(o_ref.dtype)

    @pl.when(jnp.logical_not(is_rope))
    def _():
        o_ref[...] = x_ref[...].astype(o_ref.dtype)


def _head_prep(z, cos, sin, S, tr=512):
    T = z.shape[0]
    nps = S // tr
    return pl.pallas_call(
        _prep_body,
        grid=(T // tr, COL_MERGE // PREP_TN),
        in_specs=[pl.BlockSpec((tr, PREP_TN), lambda i, j: (i, j)),
                  pl.BlockSpec((tr, HEAD_DIM), lambda i, j: (i % nps, 0)),
                  pl.BlockSpec((tr, HEAD_DIM), lambda i, j: (i % nps, 0))],
        out_specs=pl.BlockSpec((tr, PREP_TN), lambda i, j: (i, j)),
        out_shape=jax.ShapeDtypeStruct((T, COL_MERGE), _CD),
        compiler_params=_params(("parallel", "parallel")),
        name="head_prep",
    )(z, cos, sin)


def _compress_body(x_ref, pe_ref, w1_ref, w2_ref, cos_ref, sin_ref, o_ref, xpad_ref, *, S, nc):
    ncp = S // NSA_CMP_STRIDE
    xpad_ref[0:S, :] = x_ref[0]
    xpad_ref[S:S + NSA_CMP_LEN, :] = jnp.zeros((NSA_CMP_LEN, HEAD_DIM), _F32)
    acc = jnp.zeros((ncp, HEAD_DIM), _F32)
    for l in range(NSA_CMP_LEN):
        xl = xpad_ref[pl.ds(l, ncp, stride=NSA_CMP_STRIDE), :] + pe_ref[0, l:l + 1, :]
        acc = acc + _dot(xl.astype(_CD), w1_ref[0, l].astype(_CD))
    hid = jax.nn.gelu(acc)
    out = _dot(hid.astype(_CD), w2_ref[0].astype(_CD))
    is_key = pl.program_id(1) == 0
    out = jnp.where(is_key, _rope(out, cos_ref[...], sin_ref[...]), out)
    row = lax.broadcasted_iota(jnp.int32, (ncp, HEAD_DIM), 0)
    o_ref[0, 0, 0] = jnp.where(row < nc, out, 0.0).astype(o_ref.dtype)


def _nsa_compress(z3, pe, w1, w2, ccos, csin):
    B, S, _ = z3.shape
    G = NSA_KV_HEADS
    ncp = S // NSA_CMP_STRIDE
    nc = (S - NSA_CMP_LEN) // NSA_CMP_STRIDE + 1
    cb = COL_KC // HEAD_DIM
    return pl.pallas_call(
        functools.partial(_compress_body, S=S, nc=nc),
        grid=(B, 2, G),
        in_specs=[pl.BlockSpec((1, S, HEAD_DIM), lambda b, t, g: (b, 0, cb + t * G + g)),
                  pl.BlockSpec((1, NSA_CMP_LEN, HEAD_DIM), lambda b, t, g: (t, 0, 0)),
                  pl.BlockSpec((1, NSA_CMP_LEN, HEAD_DIM, HEAD_DIM), lambda b, t, g: (t, 0, 0, 0)),
                  pl.BlockSpec((1, HEAD_DIM, HEAD_DIM), lambda b, t, g: (t, 0, 0)),
                  pl.BlockSpec((ncp, HEAD_DIM), lambda b, t, g: (0, 0)),
                  pl.BlockSpec((ncp, HEAD_DIM), lambda b, t, g: (0, 0))],
        out_specs=pl.BlockSpec((1, 1, 1, ncp, HEAD_DIM), lambda b, t, g: (b, t, g, 0, 0)),
        out_shape=jax.ShapeDtypeStruct((B, 2, G, ncp, HEAD_DIM), _CD),
        scratch_shapes=[pltpu.VMEM((S + NSA_CMP_LEN, HEAD_DIM), _F32)],
        compiler_params=_params(("parallel", "parallel", "parallel")),
        name="nsa_compress",
    )(z3, pe, w1, w2, ccos, csin)


NSA_TQ = 128
NSA_TK = 128


def _online_step(s, allowed, v, m, l, acc):
    s = jnp.where(allowed, s, NEG_INF)
    m_new = jnp.maximum(m, jnp.max(s, axis=1, keepdims=True))
    alpha = jnp.exp(m - m_new)
    p = jnp.where(allowed, jnp.exp(s - m_new), 0.0)
    l = alpha * l + jnp.sum(p, axis=1, keepdims=True)
    acc = alpha * acc + _dot(p.astype(_CD), v)
    return m_new, l, acc


def _nsa_body(q_ref, ks_ref, vs_ref, kw_ref, vw_ref, kc_ref, vc_ref, gate_ref, ovt_ref, o_ref, imp_ref,
              *, S, gate_lane0):
    TQ, TK, H = NSA_TQ, NSA_TK, NSA_HPG
    R = H * TQ
    g = pl.program_id(1)
    qi = pl.program_id(2)
    t0 = qi * TQ
    q = q_ref[0]
    qs = jnp.concatenate([q[:, h * HEAD_DIM:(h + 1) * HEAD_DIM] for h in range(H)], axis=0)
    pos_r = t0 + (lax.broadcasted_iota(jnp.int32, (R, 1), 0) & (TQ - 1))

    kc = kc_ref[0, 0, 0]
    vc = vc_ref[0, 0, 0]
    ncp = kc.shape[0]
    s_c = _dot_nt(qs, kc) * SCALE
    cmp_end = lax.broadcasted_iota(jnp.int32, (1, ncp), 1) * NSA_CMP_STRIDE + (NSA_CMP_LEN - 1)
    valid = cmp_end <= pos_r
    s_c = jnp.where(valid, s_c, NEG_INF)
    p = jnp.where(valid, jnp.exp(s_c - jnp.max(s_c, axis=1, keepdims=True)), 0.0)
    lsum = jnp.sum(p, axis=1, keepdims=True)
    p_c = p / jnp.where(lsum > 0.0, lsum, 1.0)
    o_c = _dot(p_c.astype(_CD), vc)

    p_g = p_c[0:TQ]
    for h in range(1, H):
        p_g = p_g + p_c[h * TQ:(h + 1) * TQ]
    p_hi, p_lo = _split(p_g)
    ovt = ovt_ref[...]
    imp = _dot_nt(ovt, p_hi) + _dot_nt(ovt, p_lo)
    j_io = lax.broadcasted_iota(jnp.int32, (SEL_PAD, TQ), 0)
    cur = (t0 + lax.broadcasted_iota(jnp.int32, (SEL_PAD, TQ), 1)) // NSA_SEL_BLOCK
    forced = (j_io == 0) | (j_io == cur) | (j_io == cur - 1)
    imp = jnp.where(j_io <= cur, jnp.where(forced, FORCE_SCORE, imp), NEG_INF)
    imp_ref[...] = imp

    def rank_step(m, cnt):
        row = imp_ref[pl.ds(m, 1), :]
        ahead = (row > imp) | ((row == imp) & (j_io > m))
        return cnt + jnp.where(ahead, 1.0, 0.0)

    n_cand = (t0 + TQ - 1) // NSA_SEL_BLOCK + 1
    rank = lax.fori_loop(0, n_cand, rank_step, jnp.zeros((SEL_PAD, TQ), _F32))
    sel_t = (rank < float(NSA_TOP_N)).astype(_F32)
    sel = jnp.transpose(sel_t)
    sel_r = jnp.concatenate([sel] * H, axis=0).astype(_CD)

    e_row = lax.broadcasted_iota(jnp.int32, (SEL_PAD, TK), 0)
    e_col = lax.broadcasted_iota(jnp.int32, (SEL_PAD, TK), 1) // NSA_SEL_BLOCK
    k_io = lax.broadcasted_iota(jnp.int32, (1, TK), 1)
    init = (jnp.full((R, 1), NEG_INF, _F32), jnp.zeros((R, 1), _F32), jnp.zeros((R, HEAD_DIM), _F32))

    def sel_step(kb, carry):
        k0 = pl.multiple_of(kb * TK, TK)
        s = _dot_nt(qs, ks_ref[0, pl.ds(k0, TK), :]) * SCALE
        expand = (e_row == e_col + kb * (TK // NSA_SEL_BLOCK)).astype(_CD)
        allowed = (_dot(sel_r, expand) > 0.5) & ((k0 + k_io) <= pos_r)
        return _online_step(s, allowed, vs_ref[0, pl.ds(k0, TK), :], *carry)

    _, l_s, acc_s = lax.fori_loop(0, qi + 1, sel_step, init)
    o_s = acc_s / l_s

    def win_step(kb, carry):
        k0 = pl.multiple_of(kb * TK, TK)
        s = _dot_nt(qs, kw_ref[0, pl.ds(k0, TK), :]) * SCALE
        dist = pos_r - (k0 + k_io)
        allowed = (dist >= 0) & (dist < NSA_WINDOW)
        return _online_step(s, allowed, vw_ref[0, pl.ds(k0, TK), :], *carry)

    _, l_w, acc_w = lax.fori_loop(jnp.maximum(qi - NSA_WINDOW // TK, 0), qi + 1, win_step, init)
    o_w = acc_w / l_w

    gates = jax.nn.sigmoid(gate_ref[0])
    lane = lax.broadcasted_iota(jnp.int32, gates.shape, 1)
    for h in range(H):
        rows = slice(h * TQ, (h + 1) * TQ)
        base = gate_lane0 + (g * H + h) * 3
        gc, gs, gw = [jnp.sum(jnp.where(lane == base + c, gates, 0.0), axis=1, keepdims=True) for c in range(3)]
        out = gc * o_c[rows] + gs * o_s[rows] + gw * o_w[rows]
        o_ref[0, :, h * HEAD_DIM:(h + 1) * HEAD_DIM] = out.astype(o_ref.dtype)


def _overlap_t(S):
    ncp = S // NSA_CMP_STRIDE
    nc = (S - NSA_CMP_LEN) // NSA_CMP_STRIDE + 1
    ns = S // NSA_SEL_BLOCK
    cs = np.arange(nc)[:, None] * NSA_CMP_STRIDE
    ss = np.arange(ns)[None, :] * NSA_SEL_BLOCK
    ov = np.clip(np.minimum(cs + NSA_CMP_LEN, ss + NSA_SEL_BLOCK) - np.maximum(cs, ss), 0, None) / NSA_CMP_LEN
    out = np.zeros((SEL_PAD, ncp), np.float32)
    out[:ns, :nc] = ov.T
    return out


def _nsa_attention(zb3, kvc, z3, gate_col, S):
    B = zb3.shape[0]
    G, TQ = NSA_KV_HEADS, NSA_TQ
    assert S // NSA_SEL_BLOCK <= SEL_PAD and S % TQ == 0
    ncp = S // NSA_CMP_STRIDE
    qw = NSA_HPG * HEAD_DIM
    ovt = jnp.asarray(_overlap_t(S)).astype(_CD)
    resident = lambda col: pl.BlockSpec((1, S, HEAD_DIM), lambda b, g, i: (b, 0, col // HEAD_DIM + g))
    return pl.pallas_call(
        functools.partial(_nsa_body, S=S, gate_lane0=gate_col % LANES),
        grid=(B, G, S // TQ),
        in_specs=[pl.BlockSpec((1, TQ, qw), lambda b, g, i: (b, i, g)),
                  resident(COL_KS), resident(COL_VS), resident(COL_KW), resident(COL_VW),
                  pl.BlockSpec((1, 1, 1, ncp, HEAD_DIM), lambda b, g, i: (b, 0, g, 0, 0)),
                  pl.BlockSpec((1, 1, 1, ncp, HEAD_DIM), lambda b, g, i: (b, 1, g, 0, 0)),
                  pl.BlockSpec((1, TQ, LANES), lambda b, g, i: (b, i, gate_col // LANES)),
                  pl.BlockSpec((SEL_PAD, ncp), lambda b, g, i: (0, 0))],
        out_specs=pl.BlockSpec((1, TQ, qw), lambda b, g, i: (b, i, g)),
        out_shape=jax.ShapeDtypeStruct((B, S, NSA_Q_W), _CD),
        scratch_shapes=[pltpu.VMEM((SEL_PAD, TQ), _F32)],
        compiler_params=_params(("parallel", "parallel", "arbitrary")),
        name="nsa_attention",
    )(zb3, zb3, zb3, zb3, zb3, kvc, kvc, z3, ovt)


def _moba_body(q_ref, k_ref, v_ref, o_ref, kmean_ref, gate_ref, *, S):
    TQ = MOBA_BLOCK
    nf = S // MOBA_BLOCK
    nfp = kmean_ref.shape[0]
    qi = pl.program_id(2)
    t0 = pl.multiple_of(qi * TQ, TQ)

    @pl.when(qi == 0)
    def _():
        kmean_ref[...] = jnp.zeros(kmean_ref.shape, _F32)
        for n in range(nf):
            kb = k_ref[0, n * MOBA_BLOCK:(n + 1) * MOBA_BLOCK, :].astype(_F32)
            kmean_ref[n:n + 1, :] = jnp.mean(kb, axis=0, keepdims=True)

    q = q_ref[0]
    km_hi, km_lo = _split(kmean_ref[...])
    gate = _dot_nt(km_hi, q) + _dot_nt(km_lo, q)
    n_io = lax.broadcasted_iota(jnp.int32, (nfp, TQ), 0)
    gate = jnp.where(n_io < qi, gate, NEG_INF)
    gate_ref[...] = gate

    def rank_step(m, cnt):
        row = gate_ref[pl.ds(m, 1), :]
        ahead = (row > gate) | ((row == gate) & (n_io > m))
        return cnt + jnp.where(ahead, 1.0, 0.0)

    rank = lax.fori_loop(0, qi, rank_step, jnp.zeros((nfp, TQ), _F32))
    sel_t = ((rank < float(MOBA_TOPK)) & (n_io < qi)).astype(_F32)
    sel_t = jnp.concatenate([sel_t, jnp.zeros((LANES - nfp, TQ), _F32)], axis=0)
    sel = jnp.transpose(sel_t)
    lane = lax.broadcasted_iota(jnp.int32, sel.shape, 1)

    pos = t0 + lax.broadcasted_iota(jnp.int32, (TQ, 1), 0)
    k_io = lax.broadcasted_iota(jnp.int32, (1, MOBA_BLOCK), 1)
    init = (jnp.full((TQ, 1), NEG_INF, _F32), jnp.zeros((TQ, 1), _F32), jnp.zeros((TQ, HEAD_DIM), _F32))
    s_own = _dot_nt(q, k_ref[0, pl.ds(t0, TQ), :]) * SCALE
    carry = _online_step(s_own, (t0 + k_io) <= pos, v_ref[0, pl.ds(t0, TQ), :], *init)

    def past_step(n, carry):
        k0 = pl.multiple_of(n * MOBA_BLOCK, MOBA_BLOCK)
        s = _dot_nt(q, k_ref[0, pl.ds(k0, MOBA_BLOCK), :]) * SCALE
        chosen = jnp.max(jnp.where(lane == n, sel, 0.0), axis=1, keepdims=True) > 0.5
        allowed = jnp.broadcast_to(chosen, s.shape)
        return _online_step(s, allowed, v_ref[0, pl.ds(k0, MOBA_BLOCK), :], *carry)

    _, l, acc = lax.fori_loop(0, qi, past_step, carry)
    o_ref[0] = (acc / l).astype(o_ref.dtype)


def _moba_attention(zb3, S):
    B = zb3.shape[0]
    H, TQ = MOBA_HEADS, MOBA_BLOCK
    assert S % MOBA_BLOCK == 0 and S // MOBA_BLOCK <= LANES
    nfp = -(-(S // MOBA_BLOCK) // 8) * 8
    head = lambda col, rows: pl.BlockSpec((1, rows, HEAD_DIM), (
        (lambda b, h, i: (b, i, col // HEAD_DIM + h)) if rows == TQ else (lambda b, h, i: (b, 0, col // HEAD_DIM + h))))
    return pl.pallas_call(
        functools.partial(_moba_body, S=S),
        grid=(B, H, S // TQ),
        in_specs=[head(COL_QB, TQ), head(COL_KB, S), head(COL_VB, S)],
        out_specs=pl.BlockSpec((1, TQ, HEAD_DIM), lambda b, h, i: (b, i, h)),
        out_shape=jax.ShapeDtypeStruct((B, S, MOBA_W), _CD),
        scratch_shapes=[pltpu.VMEM((nfp, HEAD_DIM), _F32), pltpu.VMEM((nfp, TQ), _F32)],
        compiler_params=_params(("parallel", "parallel", "arbitrary")),
        name="moba_attention",
    )(zb3, zb3, zb3)


def _router_body(h_ref, g_ref, wr_ref, f_ref, eid_ref, wts_ref):
    x = h_ref[...]
    y = x * lax.rsqrt(jnp.mean(x * x, axis=-1, keepdims=True) + NORM_EPS)
    f = y * g_ref[...]
    f_ref[...] = f
    f_hi, f_lo = _split(f)
    w_hi, w_lo = _split(wr_ref[...])
    logits = _dot(f_hi, w_hi) + (_dot(f_hi, w_lo) + _dot(f_lo, w_hi))
    lane = lax.broadcasted_iota(jnp.int32, logits.shape, 1)
    big = jnp.int32(2 * LANES)
    in_g = lane < MOE_GROUPS
    lg = jnp.where(in_g, logits, -jnp.inf)
    mg = jnp.max(lg, axis=1, keepdims=True)
    pg_top = 1.0 / jnp.sum(jnp.where(in_g, jnp.exp(logits - mg), 0.0), axis=1, keepdims=True)
    grp = jnp.min(jnp.where(lg == mg, lane, big), axis=1, keepdims=True)
    eidx = lane - MOE_GROUPS
    in_e = (eidx >= 0) & (eidx < MOE_EXPERTS) & ((eidx // MOE_EXPERTS_PER_GROUP) == grp)
    le = jnp.where(in_e, logits, -jnp.inf)
    m1 = jnp.max(le, axis=1, keepdims=True)
    i1 = jnp.min(jnp.where(le == m1, lane, big), axis=1, keepdims=True)
    le2 = jnp.where(lane == i1, -jnp.inf, le)
    m2 = jnp.max(le2, axis=1, keepdims=True)
    i2 = jnp.min(jnp.where(le2 == m2, lane, big), axis=1, keepdims=True)
    e2 = jnp.exp(m2 - m1)
    w1 = pg_top / (1.0 + e2)
    w2 = pg_top * e2 / (1.0 + e2)
    eid_ref[...] = jnp.where(lane == 0, i1 - MOE_GROUPS, jnp.where(lane == 1, i2 - MOE_GROUPS, 0))
    wts_ref[...] = jnp.where(lane == 0, w1, jnp.where(lane == 1, w2, 0.0))


def _norm_router(h, g, wr, tm=256):
    T, D = h.shape
    row = lambda w: pl.BlockSpec((tm, w), lambda i: (i, 0))
    return pl.pallas_call(
        _router_body,
        grid=(T // tm,),
        in_specs=[row(D), pl.BlockSpec((1, D), lambda i: (0, 0)), pl.BlockSpec((D, LANES), lambda i: (0, 0))],
        out_specs=[row(D), row(LANES), row(LANES)],
        out_shape=[jax.ShapeDtypeStruct((T, D), _F32), jax.ShapeDtypeStruct((T, LANES), jnp.int32),
                   jax.ShapeDtypeStruct((T, LANES), _F32)],
        compiler_params=_params(("parallel",)),
        name="ffn_norm_router",
    )(h, g.reshape(1, D), wr)


MOE_ROWS = 256
MOE_TF = 256


def _expert_body(blk_e_ref, nused_ref, src_ref, f_hbm, w_ref, wg_ref, wu_ref, wd_ref, y_hbm,
                 xbuf, obuf, sem_in, sem_out, *, T, n_ff):
    i = pl.program_id(0)
    j = pl.program_id(1)
    n_assign = T * MOE_TOPK
    base = i * MOE_ROWS
    used = i < nused_ref[0]

    def gather_copy(r):
        tok = jnp.minimum(src_ref[base + r] // MOE_TOPK, T - 1)
        return pltpu.make_async_copy(f_hbm.at[pl.ds(tok, 1)], xbuf.at[pl.ds(r, 1)], sem_in)

    def scatter_copy(r):
        return pltpu.make_async_copy(obuf.at[pl.ds(r, 1)], y_hbm.at[pl.ds(src_ref[base + r], 1)], sem_out)

    @pl.when(used & (j == 0))
    def _():
        def start(r, c):
            gather_copy(r).start()
            return c
        lax.fori_loop(0, MOE_ROWS, start, 0)

        def wait(r, c):
            gather_copy(r).wait()
            return c
        lax.fori_loop(0, MOE_ROWS, wait, 0)
        obuf[...] = jnp.zeros(obuf.shape, _F32)

    @pl.when(used)
    def _():
        x = xbuf[...].astype(_CD)
        gate = _dot(x, wg_ref[0])
        up = _dot(x, wu_ref[0])
        hb = (jax.nn.silu(gate) * up).astype(_CD)
        obuf[...] += _dot(hb, wd_ref[0])

    @pl.when(used & (j == n_ff - 1))
    def _():
        obuf[...] = obuf[...] * w_ref[...]

        def start(r, c):
            @pl.when(src_ref[base + r] < n_assign)
            def _():
                scatter_copy(r).start()
            return c
        lax.fori_loop(0, MOE_ROWS, start, 0)

        def wait(r, c):
            @pl.when(src_ref[base + r] < n_assign)
            def _():
                scatter_copy(r).wait()
            return c
        lax.fori_loop(0, MOE_ROWS, wait, 0)


def _experts(f, blk_e, nused, buf_src, buf_w, wg, wu, wd):
    T, D = f.shape
    E, _, FF = wg.shape
    n_blocks = blk_e.shape[0]
    n_ff = FF // MOE_TF
    ff_idx = lambda i, j, nu: jnp.where(i < nu[0], j, n_ff - 1)
    grid_spec = pltpu.PrefetchScalarGridSpec(
        num_scalar_prefetch=3,
        grid=(n_blocks, n_ff),
        in_specs=[pl.BlockSpec(memory_space=pl.ANY),
                  pl.BlockSpec((MOE_ROWS, 1), lambda i, j, be, nu, sr: (i, 0)),
                  pl.BlockSpec((1, D, MOE_TF), lambda i, j, be, nu, sr: (be[i], 0, ff_idx(i, j, nu))),
                  pl.BlockSpec((1, D, MOE_TF), lambda i, j, be, nu, sr: (be[i], 0, ff_idx(i, j, nu))),
                  pl.BlockSpec((1, MOE_TF, D), lambda i, j, be, nu, sr: (be[i], ff_idx(i, j, nu), 0))],
        out_specs=pl.BlockSpec(memory_space=pl.ANY),
        scratch_shapes=[pltpu.VMEM((MOE_ROWS, D), _F32), pltpu.VMEM((MOE_ROWS, D), _F32),
                        pltpu.SemaphoreType.DMA, pltpu.SemaphoreType.DMA],
    )
    return pl.pallas_call(
        functools.partial(_expert_body, T=T, n_ff=n_ff),
        grid_spec=grid_spec,
        out_shape=jax.ShapeDtypeStruct((T * MOE_TOPK, D), _F32),
        compiler_params=_params(("arbitrary", "arbitrary")),
        name="moe_experts",
    )(blk_e, nused, buf_src, f, buf_w, wg, wu, wd)


def _group_by_expert(eid, wts):
    T = eid.shape[0]
    n_assign = T * MOE_TOPK
    n_blocks = -(-n_assign // MOE_ROWS) + MOE_EXPERTS
    e_flat = eid.reshape(-1)
    order = jnp.argsort(e_flat, stable=True).astype(jnp.int32)
    e_s = e_flat[order]
    counts = jnp.bincount(e_flat, length=MOE_EXPERTS).astype(jnp.int32)
    starts = jnp.cumsum(counts) - counts
    pcounts = (counts + MOE_ROWS - 1) // MOE_ROWS * MOE_ROWS
    pends = jnp.cumsum(pcounts)
    pstarts = pends - pcounts
    dest = pstarts[e_s] + (jnp.arange(n_assign, dtype=jnp.int32) - starts[e_s])
    buf_src = jnp.full((n_blocks * MOE_ROWS,), n_assign, jnp.int32).at[dest].set(order)
    buf_w = jnp.zeros((n_blocks * MOE_ROWS,), _F32).at[dest].set(wts.reshape(-1)[order])
    blk_e = jnp.minimum(jnp.searchsorted(pends, jnp.arange(n_blocks, dtype=jnp.int32) * MOE_ROWS, side="right"),
                        MOE_EXPERTS - 1).astype(jnp.int32)
    nused = (pends[-1:] // MOE_ROWS).astype(jnp.int32)
    return blk_e, nused, buf_src, buf_w.reshape(-1, 1)


def _addnorm_body(h_ref, y_ref, g_ref, h2_ref, n_ref):
    D = h_ref.shape[1]
    h2 = h_ref[...] + (y_ref[:, 0:D] + y_ref[:, D:2 * D])
    h2_ref[...] = h2
    y = h2 * lax.rsqrt(jnp.mean(h2 * h2, axis=-1, keepdims=True) + NORM_EPS)
    n_ref[...] = (y * g_ref[...]).astype(n_ref.dtype)


def _add_norm(h, y2, g, tm=256):
    T, D = h.shape
    return pl.pallas_call(
        _addnorm_body,
        grid=(T // tm,),
        in_specs=[pl.BlockSpec((tm, D), lambda i: (i, 0)),
                  pl.BlockSpec((tm, MOE_TOPK * D), lambda i: (i, 0)),
                  pl.BlockSpec((1, D), lambda i: (0, 0))],
        out_specs=[pl.BlockSpec((tm, D), lambda i: (i, 0)), pl.BlockSpec((tm, D), lambda i: (i, 0))],
        out_shape=[jax.ShapeDtypeStruct((T, D), _F32), jax.ShapeDtypeStruct((T, D), _CD)],
        compiler_params=_params(("parallel",)),
        name="moe_add_norm",
    )(h, y2.reshape(T, MOE_TOPK * D), g.reshape(1, D))


def kernel(x, p, attn_norm, w_in, nsa_cmp_pe, nsa_cmp_w1, nsa_cmp_w2, w_up_nsa, w_up_moba, w_out, ffn_norm, router_group, router_expert, expert_w_gate, expert_w_up, expert_w_down, ple_norm, ple_gate, ple_proj, final_norm):
    B, S, D = x.shape
    depth = w_in.shape[0]
    T = B * S
    tm = min(512, T)
    tn = min(512, D)
    assert D % tn == 0 and T % tm == 0 and S % 512 == 0

    o_q, o_kv = 0, NSA_Q_W
    o_gate = o_kv + 6 * NSA_KV_W
    o_qkvb = o_gate + NSA_GATE_W
    o_merge = o_qkvb + 3 * MOBA_W
    col_gate = COL_MERGE + 2 * D
    n_z = col_gate + GATE_PAD

    pos = jnp.arange(S)
    cos, sin = _rope_tables(pos)
    nc = (S - NSA_CMP_LEN) // NSA_CMP_STRIDE + 1
    cmp_pos = jnp.arange(S // NSA_CMP_STRIDE) * NSA_CMP_STRIDE + NSA_CMP_LEN - 1
    ccos, csin = _rope_tables(cmp_pos)

    h = x.reshape(T, D)
    for i in range(depth):
        wi = w_in[i]
        w_cat = jnp.concatenate(
            [wi[:, o_q:o_gate], wi[:, o_qkvb:o_merge + 2 * D], wi[:, o_gate:o_qkvb],
             jnp.zeros((D, GATE_PAD - NSA_GATE_W), wi.dtype)], axis=1).astype(_CD)
        a = _rmsnorm(h, attn_norm[i], _CD)
        z = _mm_call(_mm_plain_body, [a], [w_cat], [], n_z, _F32, tm, 512, "in_proj")
        zb = _head_prep(z, cos, sin, S)
        z3 = z.reshape(B, S, n_z)
        zb3 = zb.reshape(B, S, COL_MERGE)
        kvc = _nsa_compress(z3, nsa_cmp_pe[i], nsa_cmp_w1[i], nsa_cmp_w2[i], ccos, csin)
        o_a = _nsa_attention(zb3, kvc, z3, col_gate, S).reshape(T, NSA_Q_W)
        o_b = _moba_attention(zb3, S).reshape(T, MOBA_W)
        mixed = _mm_call(_mm_gated_body, [o_a, o_b], [w_up_nsa[i].astype(_CD), w_up_moba[i].astype(_CD)],
                         [(z, COL_MERGE), (z, COL_MERGE + D)], D, _CD, tm, tn, "merge_up")
        h = _mm_call(_mm_resid_body, [mixed], [w_out[i].astype(_CD)], [(h, 0)], D, _F32, tm, tn, "out_proj")

        wr = jnp.concatenate([router_group[i], router_expert[i],
                              jnp.zeros((D, LANES - MOE_GROUPS - MOE_EXPERTS), _F32)], axis=1)
        f, eid, wts = _norm_router(h, ffn_norm[i], wr)
        blk_e, nused, buf_src, buf_w = _group_by_expert(eid[:, :MOE_TOPK], wts[:, :MOE_TOPK])
        y2 = _experts(f, blk_e, nused, buf_src, buf_w, expert_w_gate[i].astype(_CD),
                      expert_w_up[i].astype(_CD), expert_w_down[i].astype(_CD))
        h2, n = _add_norm(h, y2, ple_norm[i])
        h = _mm_call(_mm_ple_body, [n, p[i].reshape(T, -1)], [ple_gate[i].astype(_CD), ple_proj[i].astype(_CD)],
                     [(h2, 0)], D, _F32, tm, tn, "ple")
    return _rmsnorm(h, final_norm, _F32).reshape(B, S, D)
```

```python
import functools

import numpy as np
import jax
import jax.numpy as jnp
from jax import lax
from jax.experimental import pallas as pl
from jax.experimental.pallas import tpu as pltpu

HEAD_DIM = 128
ROPE_THETA = 10000.0
NORM_EPS = 1e-6
NEG_INF = -1e30
FORCE_SCORE = 1e9
SCALE = HEAD_DIM ** -0.5

NSA_Q_HEADS = 16
NSA_KV_HEADS = 4
NSA_HPG = NSA_Q_HEADS // NSA_KV_HEADS
NSA_CMP_LEN = 32
NSA_CMP_STRIDE = 16
NSA_SEL_BLOCK = 64
NSA_TOP_N = 16
NSA_WINDOW = 512
NSA_Q_W = NSA_Q_HEADS * HEAD_DIM
NSA_KV_W = NSA_KV_HEADS * HEAD_DIM
NSA_GATE_W = NSA_Q_HEADS * 3

MOBA_HEADS = 16
MOBA_BLOCK = 256
MOBA_TOPK = 3
MOBA_W = MOBA_HEADS * HEAD_DIM

MOE_GROUPS = 4
MOE_EXPERTS_PER_GROUP = 8
MOE_EXPERTS = MOE_GROUPS * MOE_EXPERTS_PER_GROUP
MOE_TOPK = 2

LANES = 128
SEL_PAD = 128
GATE_PAD = 512
VMEM_LIMIT = 48 * 1024 * 1024
VMEM_LIMIT_EXPERTS = 56 * 1024 * 1024

COL_QA = 0
COL_KS = COL_QA + NSA_Q_W
COL_VS = COL_KS + NSA_KV_W
COL_KW = COL_VS + NSA_KV_W
COL_VW = COL_KW + NSA_KV_W
COL_QB = COL_VW + NSA_KV_W
COL_KB = COL_QB + MOBA_W
COL_VB = COL_KB + MOBA_W
HEADS_W = COL_VB + MOBA_W
COL_KC = 0
COL_VC = COL_KC + NSA_KV_W
COL_MERGE = COL_VC + NSA_KV_W

_CD = jnp.bfloat16
_F32 = jnp.float32


def _dot(a, b):
    return jnp.dot(a, b, preferred_element_type=_F32)


def _split(x):
    hi = x.astype(_CD)
    lo = (x - hi.astype(_F32)).astype(_CD)
    return hi, lo


def _t(x):
    return jnp.transpose(x.astype(_F32))


def _params(sem, vmem=VMEM_LIMIT):
    return pltpu.CompilerParams(dimension_semantics=sem, vmem_limit_bytes=vmem)


def _rmsnorm_body(x_ref, g_ref, o_ref):
    x = x_ref[...]
    y = x * lax.rsqrt(jnp.mean(x * x, axis=-1, keepdims=True) + NORM_EPS)
    o_ref[...] = (y * g_ref[...]).astype(o_ref.dtype)


def _rmsnorm(x, g, out_dtype, tm=256):
    T, D = x.shape
    return pl.pallas_call(
        _rmsnorm_body,
        grid=(T // tm,),
        in_specs=[pl.BlockSpec((tm, D), lambda i: (i, 0)),
                  pl.BlockSpec((1, D), lambda i: (0, 0))],
        out_specs=pl.BlockSpec((tm, D), lambda i: (i, 0)),
        out_shape=jax.ShapeDtypeStruct((T, D), out_dtype),
        compiler_params=_params(("parallel",)),
        name="rmsnorm",
    )(x, g.reshape(1, D))


def _mm_call(body, lhs, rhs, extras, n_out, out_dtype, tm, tn, name):
    M = lhs[0].shape[0]
    in_specs = [pl.BlockSpec((tm, a.shape[1]), lambda i, j: (i, 0)) for a in lhs]
    in_specs += [pl.BlockSpec((w.shape[0], tn), lambda i, j: (0, j)) for w in rhs]
    args = list(lhs) + list(rhs)
    for arr, off in extras:
        assert off % tn == 0
        in_specs.append(pl.BlockSpec((tm, tn), functools.partial(lambda i, j, o: (i, j + o), o=off // tn)))
        args.append(arr)
    return pl.pallas_call(
        body,
        grid=(M // tm, n_out // tn),
        in_specs=in_specs,
        out_specs=pl.BlockSpec((tm, tn), lambda i, j: (i, j)),
        out_shape=jax.ShapeDtypeStruct((M, n_out), out_dtype),
        compiler_params=_params(("parallel", "parallel")),
        name=name,
    )(*args)


def _mm_plain_body(a_ref, w_ref, o_ref):
    o_ref[...] = _dot(a_ref[...], w_ref[...]).astype(o_ref.dtype)


def _mm_gated_body(a1_ref, a2_ref, w1_ref, w2_ref, m1_ref, m2_ref, o_ref):
    y1 = _dot(a1_ref[...], w1_ref[...])
    y2 = _dot(a2_ref[...], w2_ref[...])
    o_ref[...] = (jax.nn.sigmoid(m1_ref[...]) * y1 + jax.nn.sigmoid(m2_ref[...]) * y2).astype(o_ref.dtype)


def _mm_resid_body(a_ref, w_ref, r_ref, o_ref):
    o_ref[...] = r_ref[...] + _dot(a_ref[...], w_ref[...])


def _mm_ple_body(n_ref, p_ref, wg_ref, wp_ref, r_ref, o_ref):
    gate = jax.nn.sigmoid(_dot(n_ref[...], wg_ref[...]))
    emb = _dot(p_ref[...].astype(_CD), wp_ref[...])
    o_ref[...] = r_ref[...] + gate * emb


def _rope_tables(pos):
    half = HEAD_DIM // 2
    inv = jnp.power(ROPE_THETA, -jnp.arange(half, dtype=_F32) / half)
    ang = pos.astype(_F32)[:, None] * inv
    c, s = jnp.cos(ang), jnp.sin(ang)
    return jnp.concatenate([c, c], axis=-1), jnp.concatenate([-s, s], axis=-1)


def _rope(x, c, s):
    return x * c + pltpu.roll(x, HEAD_DIM // 2, 1) * s


HEADS_TN = 512
_ROPE_BLOCKS = ((COL_QA // HEADS_TN, COL_KS // HEADS_TN), (COL_KS // HEADS_TN, COL_VS // HEADS_TN),
                (COL_KW // HEADS_TN, COL_VW // HEADS_TN), (COL_QB // HEADS_TN, COL_VB // HEADS_TN))


def _mm_heads_body(a_ref, w_ref, cos_ref, sin_ref, o_ref):
    j = pl.program_id(1)
    y = _dot(a_ref[...], w_ref[...])
    is_rope = functools.reduce(jnp.logical_or, [(j >= a) & (j < b) for a, b in _ROPE_BLOCKS])

    @pl.when(is_rope)
    def _():
        c = cos_ref[...]
        s = sin_ref[...]
        for h in range(HEADS_TN // HEAD_DIM):
            sl = slice(h * HEAD_DIM, (h + 1) * HEAD_DIM)
            o_ref[:, sl] = _rope(y[:, sl], c, s).astype(o_ref.dtype)

    @pl.when(jnp.logical_not(is_rope))
    def _():
        o_ref[...] = y.astype(o_ref.dtype)


def _head_proj(a, w, cos, sin, S, tm):
    T, D = a.shape
    nps = S // tm
    return pl.pallas_call(
        _mm_heads_body,
        grid=(T // tm, HEADS_W // HEADS_TN),
        in_specs=[pl.BlockSpec((tm, D), lambda i, j: (i, 0)),
                  pl.BlockSpec((D, HEADS_TN), lambda i, j: (0, j)),
                  pl.BlockSpec((tm, HEAD_DIM), lambda i, j: (i % nps, 0)),
                  pl.BlockSpec((tm, HEAD_DIM), lambda i, j: (i % nps, 0))],
        out_specs=pl.BlockSpec((tm, HEADS_TN), lambda i, j: (i, j)),
        out_shape=jax.ShapeDtypeStruct((T, HEADS_W), _CD),
        compiler_params=_params(("parallel", "parallel")),
        name="in_proj_heads",
    )(a, w, cos, sin)


def _compress_body(x_ref, pe_ref, w1_ref, w2_ref, cos_ref, sin_ref, o_ref, xpad_ref, *, S, nc):
    ncp = S // NSA_CMP_STRIDE
    xpad_ref[0:S, :] = x_ref[0]
    xpad_ref[S:S + NSA_CMP_LEN, :] = jnp.zeros((NSA_CMP_LEN, HEAD_DIM), _F32)
    acc = jnp.zeros((ncp, HEAD_DIM), _F32)
    for l in range(NSA_CMP_LEN):
        xl = xpad_ref[pl.ds(l, ncp, stride=NSA_CMP_STRIDE), :] + pe_ref[0, l:l + 1, :]
        acc = acc + _dot(xl.astype(_CD), w1_ref[0, l].astype(_CD))
    hid = jax.nn.gelu(acc)
    out = _dot(hid.astype(_CD), w2_ref[0].astype(_CD))
    is_key = pl.program_id(1) == 0
    out = jnp.where(is_key, _rope(out, cos_ref[...], sin_ref[...]), out)
    row = lax.broadcasted_iota(jnp.int32, (ncp, HEAD_DIM), 0)
    o_ref[0, 0, 0] = jnp.where(row < nc, out, 0.0).astype(o_ref.dtype)


def _nsa_compress(zr3, pe, w1, w2, ccos, csin):
    B, S, _ = zr3.shape
    G = NSA_KV_HEADS
    ncp = S // NSA_CMP_STRIDE
    nc = (S - NSA_CMP_LEN) // NSA_CMP_STRIDE + 1
    cb = COL_KC // HEAD_DIM
    return pl.pallas_call(
        functools.partial(_compress_body, S=S, nc=nc),
        grid=(B, 2, G),
        in_specs=[pl.BlockSpec((1, S, HEAD_DIM), lambda b, t, g: (b, 0, cb + t * G + g)),
                  pl.BlockSpec((1, NSA_CMP_LEN, HEAD_DIM), lambda b, t, g: (t, 0, 0)),
                  pl.BlockSpec((1, NSA_CMP_LEN, HEAD_DIM, HEAD_DIM), lambda b, t, g: (t, 0, 0, 0)),
                  pl.BlockSpec((1, HEAD_DIM, HEAD_DIM), lambda b, t, g: (t, 0, 0)),
                  pl.BlockSpec((ncp, HEAD_DIM), lambda b, t, g: (0, 0)),
                  pl.BlockSpec((ncp, HEAD_DIM), lambda b, t, g: (0, 0))],
        out_specs=pl.BlockSpec((1, 1, 1, ncp, HEAD_DIM), lambda b, t, g: (b, t, g, 0, 0)),
        out_shape=jax.ShapeDtypeStruct((B, 2, G, ncp, HEAD_DIM), _CD),
        scratch_shapes=[pltpu.VMEM((S + NSA_CMP_LEN, HEAD_DIM), _F32)],
        compiler_params=_params(("parallel", "parallel", "parallel")),
        name="nsa_compress",
    )(zr3, pe, w1, w2, ccos, csin)


SCALE_LOG2 = SCALE * 1.4426950408889634


def _online_step_t(s2, vt, m, l, acc_ref, allowed=None):
    m_new = jnp.maximum(m, jnp.max(s2, axis=0, keepdims=True))
    alpha = jnp.exp2(m - m_new)
    p = jnp.exp2(s2 - m_new)
    if allowed is not None:
        p = jnp.where(allowed, p, 0.0)
    l = alpha * l + jnp.sum(p, axis=0, keepdims=True)
    acc_ref[...] = alpha * acc_ref[...] + _dot(vt, p.astype(_CD))
    return m_new, l


def _bias_rows(ref, first, count, rows_each, width):
    return jnp.concatenate([jnp.broadcast_to(ref[pl.ds(first + u, 1), :], (rows_each, width)) for u in range(count)],
                           axis=0)


NSA_TQ = 128
NSA_TK = 512
RANK_UNROLL = 4


def _nsa_body(q_ref, ks_ref, vs_ref, kw_ref, vw_ref, kc_ref, vc_ref, gate_ref, ovt_ref, o_ref,
              vst_ref, vwt_ref, vct_ref, imp_ref, selb_ref, gt_ref, acc_ref, out_ref, *, S, gate_lane0):
    TQ, TK, H = NSA_TQ, NSA_TK, NSA_HPG
    R = H * TQ
    g = pl.program_id(1)
    qi = pl.program_id(2)
    t0 = qi * TQ
    ncp = S // NSA_CMP_STRIDE

    @pl.when(qi == 0)
    def _():
        def tr(n, c):
            for u in range(TK // LANES):
                r0 = pl.multiple_of(n * TK + u * LANES, LANES)
                cols = slice(u * LANES, (u + 1) * LANES)
                vst_ref[n, :, cols] = _t(vs_ref[0, pl.ds(r0, LANES), :]).astype(_CD)
                vwt_ref[n, :, cols] = _t(vw_ref[0, pl.ds(r0, LANES), :]).astype(_CD)
            return c
        lax.fori_loop(0, S // TK, tr, 0)
        for n in range(ncp // LANES):
            cols = slice(n * LANES, (n + 1) * LANES)
            vct_ref[:, cols] = _t(vc_ref[0, 0, 0, cols, :]).astype(_CD)

    q = q_ref[0]
    qt = jnp.concatenate([_t(q[:, h * HEAD_DIM:(h + 1) * HEAD_DIM]).astype(_CD) for h in range(H)], axis=1)
    pos_l = t0 + (lax.broadcasted_iota(jnp.int32, (1, R), 1) & (TQ - 1))

    gt_ref[...] = jnp.transpose(jax.nn.sigmoid(gate_ref[0]))

    def gate_row(c):
        return jnp.concatenate([gt_ref[pl.ds(gate_lane0 + (g * H + h) * 3 + c, 1), :] for h in range(H)], axis=1)

    s_c = _dot(kc_ref[0, 0, 0], qt) * SCALE
    cmp_end = lax.broadcasted_iota(jnp.int32, (ncp, 1), 0) * NSA_CMP_STRIDE + (NSA_CMP_LEN - 1)
    valid = cmp_end <= pos_l
    s_c = jnp.where(valid, s_c, NEG_INF)
    p = jnp.where(valid, jnp.exp(s_c - jnp.max(s_c, axis=0, keepdims=True)), 0.0)
    lsum = jnp.sum(p, axis=0, keepdims=True)
    p_c = p / jnp.where(lsum > 0.0, lsum, 1.0)
    out_ref[...] = gate_row(0) * _dot(vct_ref[...], p_c.astype(_CD))

    p_g = p_c[:, 0:TQ]
    for h in range(1, H):
        p_g = p_g + p_c[:, h * TQ:(h + 1) * TQ]
    p_hi, p_lo = _split(p_g)
    ovt = ovt_ref[...]
    imp = _dot(ovt, p_hi) + _dot(ovt, p_lo)
    j_io = lax.broadcasted_iota(jnp.int32, (SEL_PAD, TQ), 0)
    cur = (t0 + lax.broadcasted_iota(jnp.int32, (SEL_PAD, TQ), 1)) // NSA_SEL_BLOCK
    forced = (j_io == 0) | (j_io == cur) | (j_io == cur - 1)
    imp = jnp.where(j_io <= cur, jnp.where(forced, FORCE_SCORE, imp), NEG_INF)
    imp_ref[...] = imp

    def rank_step(mm, cnt):
        for u in range(RANK_UNROLL):
            m = mm * RANK_UNROLL + u
            row = imp_ref[pl.ds(m, 1), :]
            ahead = (row > imp) | ((row == imp) & (j_io > m))
            cnt = cnt + jnp.where(ahead, 1.0, 0.0)
        return cnt

    n_cand = (t0 + TQ - 1) // NSA_SEL_BLOCK + 1
    rank = lax.fori_loop(0, (n_cand + RANK_UNROLL - 1) // RANK_UNROLL, rank_step, jnp.zeros((SEL_PAD, TQ), _F32))
    sel_bias = jnp.where(rank < float(NSA_TOP_N), 0.0, NEG_INF)
    selb_ref[...] = jnp.concatenate([sel_bias] * H, axis=1)

    key_io = lax.broadcasted_iota(jnp.int32, (TK, 1), 0)
    init = (jnp.full((1, R), NEG_INF, _F32), jnp.zeros((1, R), _F32))
    spb = TK // NSA_SEL_BLOCK
    n_full = t0 // TK

    def sel_scores(kb):
        k0 = pl.multiple_of(kb * TK, TK)
        s2 = _dot(ks_ref[0, pl.ds(k0, TK), :], qt) * SCALE_LOG2
        return s2 + _bias_rows(selb_ref, kb * spb, spb, NSA_SEL_BLOCK, R)

    acc_ref[...] = jnp.zeros(acc_ref.shape, _F32)
    causal = jnp.where((n_full * TK + key_io) <= pos_l, 0.0, NEG_INF)
    carry = _online_step_t(sel_scores(n_full) + causal, vst_ref[n_full], *init, acc_ref)
    _, l_s = lax.fori_loop(0, n_full, lambda kb, c: _online_step_t(sel_scores(kb), vst_ref[kb], *c, acc_ref), carry)
    out_ref[...] += gate_row(1) * (acc_ref[...] / l_s)

    def win_step(kb, carry):
        k0 = pl.multiple_of(kb * TK, TK)
        dist = pos_l - (k0 + key_io)
        allowed = (dist >= 0) & (dist < NSA_WINDOW)
        s2 = jnp.where(allowed, _dot(kw_ref[0, pl.ds(k0, TK), :], qt) * SCALE_LOG2, NEG_INF)
        return _online_step_t(s2, vwt_ref[kb], *carry, acc_ref, allowed=allowed)

    acc_ref[...] = jnp.zeros(acc_ref.shape, _F32)
    first = jnp.maximum((t0 - (NSA_WINDOW - 1)) // TK, 0)
    _, l_w = lax.fori_loop(first, n_full + 1, win_step, init)
    out_t = out_ref[...] + gate_row(2) * (acc_ref[...] / l_w)

    for h in range(H):
        o_ref[0, :, h * HEAD_DIM:(h + 1) * HEAD_DIM] = jnp.transpose(out_t[:, h * TQ:(h + 1) * TQ]).astype(o_ref.dtype)


def _overlap_t(S):
    ncp = S // NSA_CMP_STRIDE
    nc = (S - NSA_CMP_LEN) // NSA_CMP_STRIDE + 1
    ns = S // NSA_SEL_BLOCK
    cs = np.arange(nc)[:, None] * NSA_CMP_STRIDE
    ss = np.arange(ns)[None, :] * NSA_SEL_BLOCK
    ov = np.clip(np.minimum(cs + NSA_CMP_LEN, ss + NSA_SEL_BLOCK) - np.maximum(cs, ss), 0, None) / NSA_CMP_LEN
    out = np.zeros((SEL_PAD, ncp), np.float32)
    out[:ns, :nc] = ov.T
    return out


def _nsa_attention(zb3, kvc, zr3, gate_col, S):
    B = zb3.shape[0]
    G, TQ, TK = NSA_KV_HEADS, NSA_TQ, NSA_TK
    assert S // NSA_SEL_BLOCK <= SEL_PAD and S % TK == 0 and TK % TQ == 0 and NSA_WINDOW <= TK
    ncp = S // NSA_CMP_STRIDE
    assert ncp % LANES == 0
    qw = NSA_HPG * HEAD_DIM
    R = NSA_HPG * TQ
    ovt = jnp.asarray(_overlap_t(S)).astype(_CD)
    resident = lambda col: pl.BlockSpec((1, S, HEAD_DIM), lambda b, g, i: (b, 0, col // HEAD_DIM + g))
    return pl.pallas_call(
        functools.partial(_nsa_body, S=S, gate_lane0=gate_col % LANES),
        grid=(B, G, S // TQ),
        in_specs=[pl.BlockSpec((1, TQ, qw), lambda b, g, i: (b, i, g)),
                  resident(COL_KS), resident(COL_VS), resident(COL_KW), resident(COL_VW),
                  pl.BlockSpec((1, 1, 1, ncp, HEAD_DIM), lambda b, g, i: (b, 0, g, 0, 0)),
                  pl.BlockSpec((1, 1, 1, ncp, HEAD_DIM), lambda b, g, i: (b, 1, g, 0, 0)),
                  pl.BlockSpec((1, TQ, LANES), lambda b, g, i: (b, i, gate_col // LANES)),
                  pl.BlockSpec((SEL_PAD, ncp), lambda b, g, i: (0, 0))],
        out_specs=pl.BlockSpec((1, TQ, qw), lambda b, g, i: (b, i, g)),
        out_shape=jax.ShapeDtypeStruct((B, S, NSA_Q_W), _CD),
        scratch_shapes=[pltpu.VMEM((S // TK, HEAD_DIM, TK), _CD),
                        pltpu.VMEM((S // TK, HEAD_DIM, TK), _CD),
                        pltpu.VMEM((HEAD_DIM, ncp), _CD),
                        pltpu.VMEM((SEL_PAD, TQ), _F32),
                        pltpu.VMEM((SEL_PAD, R), _F32),
                        pltpu.VMEM((LANES, TQ), _F32),
                        pltpu.VMEM((HEAD_DIM, R), _F32),
                        pltpu.VMEM((HEAD_DIM, R), _F32)],
        compiler_params=_params(("parallel", "parallel", "arbitrary")),
        name="nsa_attention",
    )(zb3, zb3, zb3, zb3, zb3, kvc, kvc, zr3, ovt)


MOBA_NB = 4


def _moba_body(q_ref, k_ref, v_ref, o_ref, kmean_ref, gate_ref, selb_ref, vt_ref, acc_ref, *, S):
    TQ, NB = MOBA_BLOCK, MOBA_NB
    TK = NB * MOBA_BLOCK
    nf = S // MOBA_BLOCK
    nfp = kmean_ref.shape[0]
    qi = pl.program_id(2)
    t0 = pl.multiple_of(qi * TQ, TQ)

    @pl.when(qi == 0)
    def _():
        kmean_ref[...] = jnp.zeros(kmean_ref.shape, _F32)
        for n in range(nf):
            rows = slice(n * MOBA_BLOCK, (n + 1) * MOBA_BLOCK)
            kmean_ref[n:n + 1, :] = jnp.mean(k_ref[0, rows, :].astype(_F32), axis=0, keepdims=True)
            cols = slice((n % NB) * MOBA_BLOCK, (n % NB + 1) * MOBA_BLOCK)
            vt_ref[n // NB, :, cols] = _t(v_ref[0, rows, :]).astype(_CD)

    qt = _t(q_ref[0]).astype(_CD)
    km_hi, km_lo = _split(kmean_ref[...])
    gate = _dot(km_hi, qt) + _dot(km_lo, qt)
    n_io = lax.broadcasted_iota(jnp.int32, (nfp, TQ), 0)
    gate = jnp.where(n_io < qi, gate, NEG_INF)
    gate_ref[...] = gate

    def rank_step(mm, cnt):
        for u in range(RANK_UNROLL):
            m = mm * RANK_UNROLL + u
            row = gate_ref[pl.ds(m, 1), :]
            ahead = (row > gate) | ((row == gate) & (n_io > m))
            cnt = cnt + jnp.where(ahead, 1.0, 0.0)
        return cnt

    rank = lax.fori_loop(0, (qi + RANK_UNROLL - 1) // RANK_UNROLL, rank_step, jnp.zeros((nfp, TQ), _F32))
    keep = ((rank < float(MOBA_TOPK)) & (n_io < qi)) | (n_io == qi)
    selb_ref[...] = jnp.where(keep, 0.0, NEG_INF)

    pos_l = t0 + lax.broadcasted_iota(jnp.int32, (1, TQ), 1)
    key_io = lax.broadcasted_iota(jnp.int32, (TK, 1), 0)
    init = (jnp.full((1, TQ), NEG_INF, _F32), jnp.zeros((1, TQ), _F32))
    n_full = qi // NB

    def scores(kb):
        k0 = pl.multiple_of(kb * TK, TK)
        s2 = _dot(k_ref[0, pl.ds(k0, TK), :], qt) * SCALE_LOG2
        return s2 + _bias_rows(selb_ref, kb * NB, NB, MOBA_BLOCK, TQ)

    acc_ref[...] = jnp.zeros(acc_ref.shape, _F32)
    causal = jnp.where((n_full * TK + key_io) <= pos_l, 0.0, NEG_INF)
    carry = _online_step_t(scores(n_full) + causal, vt_ref[n_full], *init, acc_ref)
    _, l = lax.fori_loop(0, n_full, lambda kb, c: _online_step_t(scores(kb), vt_ref[kb], *c, acc_ref), carry)
    o_ref[0] = jnp.transpose(acc_ref[...] / l).astype(o_ref.dtype)


def _moba_attention(zb3, S):
    B = zb3.shape[0]
    H, TQ = MOBA_HEADS, MOBA_BLOCK
    assert S % MOBA_BLOCK == 0
    nf = S // MOBA_BLOCK
    nfp = -(-nf // 8) * 8
    assert nfp % RANK_UNROLL == 0 and nf % MOBA_NB == 0
    tile = lambda col: pl.BlockSpec((1, TQ, HEAD_DIM), lambda b, h, i: (b, i, col // HEAD_DIM + h))
    resident = lambda col: pl.BlockSpec((1, S, HEAD_DIM), lambda b, h, i: (b, 0, col // HEAD_DIM + h))
    return pl.pallas_call(
        functools.partial(_moba_body, S=S),
        grid=(B, H, S // TQ),
        in_specs=[tile(COL_QB), resident(COL_KB), resident(COL_VB)],
        out_specs=pl.BlockSpec((1, TQ, HEAD_DIM), lambda b, h, i: (b, i, h)),
        out_shape=jax.ShapeDtypeStruct((B, S, MOBA_W), _CD),
        scratch_shapes=[pltpu.VMEM((nfp, HEAD_DIM), _F32),
                        pltpu.VMEM((nfp, TQ), _F32),
                        pltpu.VMEM((nfp, TQ), _F32),
                        pltpu.VMEM((nf // MOBA_NB, HEAD_DIM, MOBA_NB * MOBA_BLOCK), _CD),
                        pltpu.VMEM((HEAD_DIM, TQ), _F32)],
        compiler_params=_params(("parallel", "parallel", "arbitrary")),
        name="moba_attention",
    )(zb3, zb3, zb3)


def _router_body(h_ref, g_ref, wr_ref, f_ref, eid_ref, wts_ref):
    x = h_ref[...]
    y = x * lax.rsqrt(jnp.mean(x * x, axis=-1, keepdims=True) + NORM_EPS)
    f = y * g_ref[...]
    f_ref[...] = f
    f_hi, f_lo = _split(f)
    w_hi, w_lo = _split(wr_ref[...])
    logits = _dot(f_hi, w_hi) + (_dot(f_hi, w_lo) + _dot(f_lo, w_hi))
    lane = lax.broadcasted_iota(jnp.int32, logits.shape, 1)
    big = jnp.int32(2 * LANES)
    in_g = lane < MOE_GROUPS
    lg = jnp.where(in_g, logits, -jnp.inf)
    mg = jnp.max(lg, axis=1, keepdims=True)
    pg_top = 1.0 / jnp.sum(jnp.where(in_g, jnp.exp(logits - mg), 0.0), axis=1, keepdims=True)
    grp = jnp.min(jnp.where(lg == mg, lane, big), axis=1, keepdims=True)
    eidx = lane - MOE_GROUPS
    in_e = (eidx >= 0) & (eidx < MOE_EXPERTS) & ((eidx // MOE_EXPERTS_PER_GROUP) == grp)
    le = jnp.where(in_e, logits, -jnp.inf)
    m1 = jnp.max(le, axis=1, keepdims=True)
    i1 = jnp.min(jnp.where(le == m1, lane, big), axis=1, keepdims=True)
    le2 = jnp.where(lane == i1, -jnp.inf, le)
    m2 = jnp.max(le2, axis=1, keepdims=True)
    i2 = jnp.min(jnp.where(le2 == m2, lane, big), axis=1, keepdims=True)
    e2 = jnp.exp(m2 - m1)
    w1 = pg_top / (1.0 + e2)
    w2 = pg_top * e2 / (1.0 + e2)
    eid_ref[...] = jnp.where(lane == 0, i1 - MOE_GROUPS, jnp.where(lane == 1, i2 - MOE_GROUPS, 0))
    wts_ref[...] = jnp.where(lane == 0, w1, jnp.where(lane == 1, w2, 0.0))


def _norm_router(h, g, wr, tm=256):
    T, D = h.shape
    row = lambda w: pl.BlockSpec((tm, w), lambda i: (i, 0))
    return pl.pallas_call(
        _router_body,
        grid=(T // tm,),
        in_specs=[row(D), pl.BlockSpec((1, D), lambda i: (0, 0)), pl.BlockSpec((D, LANES), lambda i: (0, 0))],
        out_specs=[row(D), row(LANES), row(LANES)],
        out_shape=[jax.ShapeDtypeStruct((T, D), _F32), jax.ShapeDtypeStruct((T, LANES), jnp.int32),
                   jax.ShapeDtypeStruct((T, LANES), _F32)],
        compiler_params=_params(("parallel",)),
        name="ffn_norm_router",
    )(h, g.reshape(1, D), wr)


MOE_ROWS = 256


def _expert_body(blk_e_ref, nused_ref, src_ref, f_hbm, w_ref, wg_ref, wu_ref, wd_ref, y_hbm,
                 xbuf, obuf, sem_in, sem_out, *, T):
    i = pl.program_id(0)
    nused = nused_ref[0]
    slot = i % 2
    n_assign = T * MOE_TOPK

    def gather(blk, s, r):
        tok = jnp.minimum(src_ref[blk * MOE_ROWS + r] // MOE_TOPK, T - 1)
        return pltpu.make_async_copy(f_hbm.at[pl.ds(tok, 1)], xbuf.at[s, pl.ds(r, 1)], sem_in.at[s])

    def scatter(blk, s, r):
        a = src_ref[blk * MOE_ROWS + r]
        dst = (a % MOE_TOPK) * T + a // MOE_TOPK
        return pltpu.make_async_copy(obuf.at[s, pl.ds(r, 1)], y_hbm.at[pl.ds(dst, 1)], sem_out.at[s])

    def for_rows(fn):
        def body(r, c):
            fn(r)
            return c
        lax.fori_loop(0, MOE_ROWS, body, 0)

    def gather_all(blk, s, wait):
        for_rows(lambda r: gather(blk, s, r).wait() if wait else gather(blk, s, r).start())

    def scatter_all(blk, s, wait):
        def one(r):
            @pl.when(src_ref[blk * MOE_ROWS + r] < n_assign)
            def _():
                if wait:
                    scatter(blk, s, r).wait()
                else:
                    scatter(blk, s, r).start()
        for_rows(one)

    @pl.when(i == 0)
    def _():
        gather_all(0, 0, wait=False)

    @pl.when(i + 1 < nused)
    def _():
        gather_all(i + 1, 1 - slot, wait=False)

    @pl.when(i < nused)
    def _():
        gather_all(i, slot, wait=True)

        @pl.when(i >= 2)
        def _():
            scatter_all(i - 2, slot, wait=True)

        x = xbuf[slot].astype(_CD)
        hb = (jax.nn.silu(_dot(x, wg_ref[0])) * _dot(x, wu_ref[0])).astype(_CD)
        obuf[slot] = _dot(hb, wd_ref[0]) * w_ref[...]
        scatter_all(i, slot, wait=False)

    @pl.when(i == nused - 1)
    def _():
        scatter_all(i, slot, wait=True)

        @pl.when(i >= 1)
        def _():
            scatter_all(i - 1, 1 - slot, wait=True)


def _experts(f, blk_e, nused, buf_src, buf_w, wg, wu, wd):
    T, D = f.shape
    E, _, FF = wg.shape
    n_blocks = blk_e.shape[0]
    once = pl.Buffered(1)
    grid_spec = pltpu.PrefetchScalarGridSpec(
        num_scalar_prefetch=3,
        grid=(n_blocks,),
        in_specs=[pl.BlockSpec(memory_space=pl.ANY),
                  pl.BlockSpec((MOE_ROWS, 1), lambda i, be, nu, sr: (i, 0)),
                  pl.BlockSpec((1, D, FF), lambda i, be, nu, sr: (be[i], 0, 0), pipeline_mode=once),
                  pl.BlockSpec((1, D, FF), lambda i, be, nu, sr: (be[i], 0, 0), pipeline_mode=once),
                  pl.BlockSpec((1, FF, D), lambda i, be, nu, sr: (be[i], 0, 0), pipeline_mode=once)],
        out_specs=pl.BlockSpec(memory_space=pl.ANY),
        scratch_shapes=[pltpu.VMEM((2, MOE_ROWS, D), _F32), pltpu.VMEM((2, MOE_ROWS, D), _F32),
                        pltpu.SemaphoreType.DMA((2,)), pltpu.SemaphoreType.DMA((2,))],
    )
    return pl.pallas_call(
        functools.partial(_expert_body, T=T),
        grid_spec=grid_spec,
        out_shape=jax.ShapeDtypeStruct((MOE_TOPK * T, D), _F32),
        compiler_params=_params(("arbitrary",), VMEM_LIMIT_EXPERTS),
        name="moe_experts",
    )(blk_e, nused, buf_src, f, buf_w, wg, wu, wd)


def _group_by_expert(eid, wts):
    T = eid.shape[0]
    n_assign = T * MOE_TOPK
    n_blocks = -(-n_assign // MOE_ROWS) + MOE_EXPERTS
    e_flat = eid.reshape(-1)
    order = jnp.argsort(e_flat, stable=True).astype(jnp.int32)
    counts = jnp.bincount(e_flat, length=MOE_EXPERTS).astype(jnp.int32)
    starts = jnp.cumsum(counts) - counts
    pcounts = (counts + MOE_ROWS - 1) // MOE_ROWS * MOE_ROWS
    pends = jnp.cumsum(pcounts)
    pstarts = pends - pcounts
    blk_start = jnp.arange(n_blocks, dtype=jnp.int32) * MOE_ROWS
    blk_e = jnp.minimum(jnp.sum(pends[None, :] <= blk_start[:, None], axis=1), MOE_EXPERTS - 1).astype(jnp.int32)
    slot = jnp.arange(n_blocks * MOE_ROWS, dtype=jnp.int32)
    e_p = jnp.repeat(blk_e, MOE_ROWS)
    off = slot - pstarts[e_p]
    valid = (off >= 0) & (off < counts[e_p])
    src = order[jnp.clip(starts[e_p] + off, 0, n_assign - 1)]
    buf_src = jnp.where(valid, src, n_assign).astype(jnp.int32)
    buf_w = jnp.where(valid, wts.reshape(-1)[src], 0.0)
    nused = (pends[-1:] // MOE_ROWS).astype(jnp.int32)
    return blk_e, nused, buf_src, buf_w.reshape(-1, 1)


def _addnorm_body(h_ref, y0_ref, y1_ref, g_ref, h2_ref, n_ref):
    h2 = h_ref[...] + (y0_ref[0] + y1_ref[0])
    h2_ref[...] = h2
    y = h2 * lax.rsqrt(jnp.mean(h2 * h2, axis=-1, keepdims=True) + NORM_EPS)
    n_ref[...] = (y * g_ref[...]).astype(n_ref.dtype)


def _add_norm(h, y2, g, tm=256):
    T, D = h.shape
    y3 = y2.reshape(MOE_TOPK, T, D)
    return pl.pallas_call(
        _addnorm_body,
        grid=(T // tm,),
        in_specs=[pl.BlockSpec((tm, D), lambda i: (i, 0)),
                  pl.BlockSpec((1, tm, D), lambda i: (0, i, 0)),
                  pl.BlockSpec((1, tm, D), lambda i: (1, i, 0)),
                  pl.BlockSpec((1, D), lambda i: (0, 0))],
        out_specs=[pl.BlockSpec((tm, D), lambda i: (i, 0)), pl.BlockSpec((tm, D), lambda i: (i, 0))],
        out_shape=[jax.ShapeDtypeStruct((T, D), _F32), jax.ShapeDtypeStruct((T, D), _CD)],
        compiler_params=_params(("parallel",)),
        name="moe_add_norm",
    )(h, y3, y3, g.reshape(1, D))


def kernel(x, p, attn_norm, w_in, nsa_cmp_pe, nsa_cmp_w1, nsa_cmp_w2, w_up_nsa, w_up_moba, w_out, ffn_norm, router_group, router_expert, expert_w_gate, expert_w_up, expert_w_down, ple_norm, ple_gate, ple_proj, final_norm):
    B, S, D = x.shape
    depth = w_in.shape[0]
    T = B * S
    tm = min(512, T)
    tn = min(512, D)
    assert D % tn == 0 and T % tm == 0 and S % tm == 0

    o_kv = NSA_Q_W
    o_gate = o_kv + 6 * NSA_KV_W
    o_qkvb = o_gate + NSA_GATE_W
    o_merge = o_qkvb + 3 * MOBA_W
    col_gate = COL_MERGE + 2 * D
    n_rest = col_gate + GATE_PAD

    cos, sin = _rope_tables(jnp.arange(S))
    ccos, csin = _rope_tables(jnp.arange(S // NSA_CMP_STRIDE) * NSA_CMP_STRIDE + NSA_CMP_LEN - 1)

    h = x.reshape(T, D)
    for i in range(depth):
        wi = w_in[i]
        w_heads = jnp.concatenate([wi[:, 0:o_kv], wi[:, o_kv + 2 * NSA_KV_W:o_gate], wi[:, o_qkvb:o_merge]],
                                  axis=1).astype(_CD)
        w_rest = jnp.concatenate([wi[:, o_kv:o_kv + 2 * NSA_KV_W], wi[:, o_merge:o_merge + 2 * D],
                                  wi[:, o_gate:o_qkvb], jnp.zeros((D, GATE_PAD - NSA_GATE_W), wi.dtype)],
                                 axis=1).astype(_CD)
        a = _rmsnorm(h, attn_norm[i], _CD)
        zb3 = _head_proj(a, w_heads, cos, sin, S, tm).reshape(B, S, HEADS_W)
        zr = _mm_call(_mm_plain_body, [a], [w_rest], [], n_rest, _F32, tm, tn, "in_proj_rest")
        zr3 = zr.reshape(B, S, n_rest)
        kvc = _nsa_compress(zr3, nsa_cmp_pe[i], nsa_cmp_w1[i], nsa_cmp_w2[i], ccos, csin)
        o_a = _nsa_attention(zb3, kvc, zr3, col_gate, S).reshape(T, NSA_Q_W)
        o_b = _moba_attention(zb3, S).reshape(T, MOBA_W)
        mixed = _mm_call(_mm_gated_body, [o_a, o_b], [w_up_nsa[i].astype(_CD), w_up_moba[i].astype(_CD)],
                         [(zr, COL_MERGE), (zr, COL_MERGE + D)], D, _CD, tm, tn, "merge_up")
        h = _mm_call(_mm_resid_body, [mixed], [w_out[i].astype(_CD)], [(h, 0)], D, _F32, tm, tn, "out_proj")

        wr = jnp.concatenate([router_group[i], router_expert[i],
                              jnp.zeros((D, LANES - MOE_GROUPS - MOE_EXPERTS), _F32)], axis=1)
        f, eid, wts = _norm_router(h, ffn_norm[i], wr)
        blk_e, nused, buf_src, buf_w = _group_by_expert(eid[:, :MOE_TOPK], wts[:, :MOE_TOPK])
        y2 = _experts(f, blk_e, nused, buf_src, buf_w, expert_w_gate[i].astype(_CD),
                      expert_w_up[i].astype(_CD), expert_w_down[i].astype(_CD))
        h2, n = _add_norm(h, y2, ple_norm[i])
        h = _mm_call(_mm_ple_body, [n, p[i].reshape(T, -1)], [ple_gate[i].astype(_CD), ple_proj[i].astype(_CD)],
                     [(h2, 0)], D, _F32, tm, tn, "ple")
    return _rmsnorm(h, final_norm, _F32).reshape(B, S, D)
```

```python
import functools

import numpy as np
import jax
import jax.numpy as jnp
from jax import lax
from jax.experimental import pallas as pl
from jax.experimental.pallas import tpu as pltpu

HEAD_DIM = 128
ROPE_THETA = 10000.0
NORM_EPS = 1e-6
NEG_INF = -1e30
FORCE_SCORE = 1e9
SCALE = HEAD_DIM ** -0.5

NSA_Q_HEADS = 16
NSA_KV_HEADS = 4
NSA_HPG = NSA_Q_HEADS // NSA_KV_HEADS
NSA_CMP_LEN = 32
NSA_CMP_STRIDE = 16
NSA_SEL_BLOCK = 64
NSA_TOP_N = 16
NSA_WINDOW = 512
NSA_Q_W = NSA_Q_HEADS * HEAD_DIM
NSA_KV_W = NSA_KV_HEADS * HEAD_DIM
NSA_GATE_W = NSA_Q_HEADS * 3

MOBA_HEADS = 16
MOBA_BLOCK = 256
MOBA_TOPK = 3
MOBA_W = MOBA_HEADS * HEAD_DIM

MOE_GROUPS = 4
MOE_EXPERTS_PER_GROUP = 8
MOE_EXPERTS = MOE_GROUPS * MOE_EXPERTS_PER_GROUP
MOE_TOPK = 2

LANES = 128
SEL_PAD = 128
GATE_PAD = 512
VMEM_LIMIT = 48 * 1024 * 1024
VMEM_LIMIT_EXPERTS = 56 * 1024 * 1024

COL_QA = 0
COL_KS = COL_QA + NSA_Q_W
COL_VS = COL_KS + NSA_KV_W
COL_KW = COL_VS + NSA_KV_W
COL_VW = COL_KW + NSA_KV_W
COL_QB = COL_VW + NSA_KV_W
COL_KB = COL_QB + MOBA_W
COL_VB = COL_KB + MOBA_W
HEADS_W = COL_VB + MOBA_W
COL_KC = 0
COL_VC = COL_KC + NSA_KV_W
COL_MERGE = COL_VC + NSA_KV_W

_CD = jnp.bfloat16
_F32 = jnp.float32


def _dot(a, b):
    return jnp.dot(a, b, preferred_element_type=_F32)


def _split(x):
    hi = x.astype(_CD)
    lo = (x - hi.astype(_F32)).astype(_CD)
    return hi, lo


def _t(x):
    return jnp.transpose(x.astype(_F32))


def _params(sem, vmem=VMEM_LIMIT):
    return pltpu.CompilerParams(dimension_semantics=sem, vmem_limit_bytes=vmem)


def _rmsnorm_body(x_ref, g_ref, o_ref):
    x = x_ref[...]
    y = x * lax.rsqrt(jnp.mean(x * x, axis=-1, keepdims=True) + NORM_EPS)
    o_ref[...] = (y * g_ref[...]).astype(o_ref.dtype)


def _rmsnorm(x, g, out_dtype, tm=256):
    T, D = x.shape
    return pl.pallas_call(
        _rmsnorm_body,
        grid=(T // tm,),
        in_specs=[pl.BlockSpec((tm, D), lambda i: (i, 0)),
                  pl.BlockSpec((1, D), lambda i: (0, 0))],
        out_specs=pl.BlockSpec((tm, D), lambda i: (i, 0)),
        out_shape=jax.ShapeDtypeStruct((T, D), out_dtype),
        compiler_params=_params(("parallel",)),
        name="rmsnorm",
    )(x, g.reshape(1, D))


def _mm_call(body, lhs, rhs, extras, n_out, out_dtype, tm, tn, name):
    M = lhs[0].shape[0]
    in_specs = [pl.BlockSpec((tm, a.shape[1]), lambda i, j: (i, 0)) for a in lhs]
    in_specs += [pl.BlockSpec((w.shape[0], tn), lambda i, j: (0, j)) for w in rhs]
    args = list(lhs) + list(rhs)
    for arr, off in extras:
        assert off % tn == 0
        in_specs.append(pl.BlockSpec((tm, tn), functools.partial(lambda i, j, o: (i, j + o), o=off // tn)))
        args.append(arr)
    return pl.pallas_call(
        body,
        grid=(M // tm, n_out // tn),
        in_specs=in_specs,
        out_specs=pl.BlockSpec((tm, tn), lambda i, j: (i, j)),
        out_shape=jax.ShapeDtypeStruct((M, n_out), out_dtype),
        compiler_params=_params(("parallel", "parallel")),
        name=name,
    )(*args)


def _mm_plain_body(a_ref, w_ref, o_ref):
    o_ref[...] = _dot(a_ref[...], w_ref[...]).astype(o_ref.dtype)


def _mm_gated_body(a1_ref, a2_ref, w1_ref, w2_ref, m1_ref, m2_ref, o_ref):
    y1 = _dot(a1_ref[...], w1_ref[...])
    y2 = _dot(a2_ref[...], w2_ref[...])
    o_ref[...] = (jax.nn.sigmoid(m1_ref[...]) * y1 + jax.nn.sigmoid(m2_ref[...]) * y2).astype(o_ref.dtype)


def _mm_resid_body(a_ref, w_ref, r_ref, o_ref):
    o_ref[...] = r_ref[...] + _dot(a_ref[...], w_ref[...])


def _mm_ple_body(n_ref, p_ref, wg_ref, wp_ref, r_ref, o_ref):
    gate = jax.nn.sigmoid(_dot(n_ref[...], wg_ref[...]))
    emb = _dot(p_ref[...].astype(_CD), wp_ref[...])
    o_ref[...] = r_ref[...] + gate * emb


def _rope_tables(pos):
    half = HEAD_DIM // 2
    inv = jnp.power(ROPE_THETA, -jnp.arange(half, dtype=_F32) / half)
    ang = pos.astype(_F32)[:, None] * inv
    c, s = jnp.cos(ang), jnp.sin(ang)
    return jnp.concatenate([c, c], axis=-1), jnp.concatenate([-s, s], axis=-1)


def _rope(x, c, s):
    return x * c + pltpu.roll(x, HEAD_DIM // 2, 1) * s


HEADS_TN = 512
_ROPE_BLOCKS = ((COL_QA // HEADS_TN, COL_KS // HEADS_TN), (COL_KS // HEADS_TN, COL_VS // HEADS_TN),
                (COL_KW // HEADS_TN, COL_VW // HEADS_TN), (COL_QB // HEADS_TN, COL_VB // HEADS_TN))


def _mm_heads_body(a_ref, w_ref, cos_ref, sin_ref, o_ref):
    j = pl.program_id(1)
    y = _dot(a_ref[...], w_ref[...])
    is_rope = functools.reduce(jnp.logical_or, [(j >= a) & (j < b) for a, b in _ROPE_BLOCKS])

    @pl.when(is_rope)
    def _():
        c = cos_ref[...]
        s = sin_ref[...]
        for h in range(HEADS_TN // HEAD_DIM):
            sl = slice(h * HEAD_DIM, (h + 1) * HEAD_DIM)
            o_ref[:, sl] = _rope(y[:, sl], c, s).astype(o_ref.dtype)

    @pl.when(jnp.logical_not(is_rope))
    def _():
        o_ref[...] = y.astype(o_ref.dtype)


def _head_proj(a, w, cos, sin, S, tm):
    T, D = a.shape
    nps = S // tm
    return pl.pallas_call(
        _mm_heads_body,
        grid=(T // tm, HEADS_W // HEADS_TN),
        in_specs=[pl.BlockSpec((tm, D), lambda i, j: (i, 0)),
                  pl.BlockSpec((D, HEADS_TN), lambda i, j: (0, j)),
                  pl.BlockSpec((tm, HEAD_DIM), lambda i, j: (i % nps, 0)),
                  pl.BlockSpec((tm, HEAD_DIM), lambda i, j: (i % nps, 0))],
        out_specs=pl.BlockSpec((tm, HEADS_TN), lambda i, j: (i, j)),
        out_shape=jax.ShapeDtypeStruct((T, HEADS_W), _CD),
        compiler_params=_params(("parallel", "parallel")),
        name="in_proj_heads",
    )(a, w, cos, sin)


def _compress_body(x_ref, pe_ref, w1_ref, w2_ref, cos_ref, sin_ref, o_ref, xpad_ref, *, S, nc):
    ncp = S // NSA_CMP_STRIDE
    xpad_ref[0:S, :] = x_ref[0]
    xpad_ref[S:S + NSA_CMP_LEN, :] = jnp.zeros((NSA_CMP_LEN, HEAD_DIM), _F32)
    acc = jnp.zeros((ncp, HEAD_DIM), _F32)
    for l in range(NSA_CMP_LEN):
        xl = xpad_ref[pl.ds(l, ncp, stride=NSA_CMP_STRIDE), :] + pe_ref[0, l:l + 1, :]
        acc = acc + _dot(xl.astype(_CD), w1_ref[0, l].astype(_CD))
    hid = jax.nn.gelu(acc)
    out = _dot(hid.astype(_CD), w2_ref[0].astype(_CD))
    is_key = pl.program_id(1) == 0
    out = jnp.where(is_key, _rope(out, cos_ref[...], sin_ref[...]), out)
    row = lax.broadcasted_iota(jnp.int32, (ncp, HEAD_DIM), 0)
    o_ref[0, 0, 0] = jnp.where(row < nc, out, 0.0).astype(o_ref.dtype)


def _nsa_compress(zr3, pe, w1, w2, ccos, csin):
    B, S, _ = zr3.shape
    G = NSA_KV_HEADS
    ncp = S // NSA_CMP_STRIDE
    nc = (S - NSA_CMP_LEN) // NSA_CMP_STRIDE + 1
    cb = COL_KC // HEAD_DIM
    return pl.pallas_call(
        functools.partial(_compress_body, S=S, nc=nc),
        grid=(B, 2, G),
        in_specs=[pl.BlockSpec((1, S, HEAD_DIM), lambda b, t, g: (b, 0, cb + t * G + g)),
                  pl.BlockSpec((1, NSA_CMP_LEN, HEAD_DIM), lambda b, t, g: (t, 0, 0)),
                  pl.BlockSpec((1, NSA_CMP_LEN, HEAD_DIM, HEAD_DIM), lambda b, t, g: (t, 0, 0, 0)),
                  pl.BlockSpec((1, HEAD_DIM, HEAD_DIM), lambda b, t, g: (t, 0, 0)),
                  pl.BlockSpec((ncp, HEAD_DIM), lambda b, t, g: (0, 0)),
                  pl.BlockSpec((ncp, HEAD_DIM), lambda b, t, g: (0, 0))],
        out_specs=pl.BlockSpec((1, 1, 1, ncp, HEAD_DIM), lambda b, t, g: (b, t, g, 0, 0)),
        out_shape=jax.ShapeDtypeStruct((B, 2, G, ncp, HEAD_DIM), _CD),
        scratch_shapes=[pltpu.VMEM((S + NSA_CMP_LEN, HEAD_DIM), _F32)],
        compiler_params=_params(("parallel", "parallel", "parallel")),
        name="nsa_compress",
    )(zr3, pe, w1, w2, ccos, csin)


SCALE_LOG2 = SCALE * 1.4426950408889634


def _online_step_t(s2, vt, m, l, acc_ref, allowed=None):
    m_new = jnp.maximum(m, jnp.max(s2, axis=0, keepdims=True))
    alpha = jnp.exp2(m - m_new)
    p = jnp.exp2(s2 - m_new)
    if allowed is not None:
        p = jnp.where(allowed, p, 0.0)
    l = alpha * l + jnp.sum(p, axis=0, keepdims=True)
    acc_ref[...] = alpha * acc_ref[...] + _dot(vt, p.astype(_CD))
    return m_new, l


def _bias_rows(ref, first, count, rows_each, width):
    return jnp.concatenate([jnp.broadcast_to(ref[pl.ds(first + u, 1), :], (rows_each, width)) for u in range(count)],
                           axis=0)


NSA_TQ = 128
NSA_TK = 512
RANK_UNROLL = 4


def _nsa_body(q_ref, ks_ref, vs_ref, kw_ref, vw_ref, kc_ref, vc_ref, gate_ref, ovt_ref, o_ref,
              vst_ref, vwt_ref, vct_ref, imp_ref, selb_ref, gt_ref, acc_ref, out_ref, *, S, gate_lane0):
    TQ, TK, H = NSA_TQ, NSA_TK, NSA_HPG
    R = H * TQ
    g = pl.program_id(1)
    qi = pl.program_id(2)
    t0 = qi * TQ
    ncp = S // NSA_CMP_STRIDE

    @pl.when(qi == 0)
    def _():
        def tr(n, c):
            for u in range(TK // LANES):
                r0 = pl.multiple_of(n * TK + u * LANES, LANES)
                cols = slice(u * LANES, (u + 1) * LANES)
                vst_ref[n, :, cols] = _t(vs_ref[0, pl.ds(r0, LANES), :]).astype(_CD)
                vwt_ref[n * (TK // LANES) + u] = _t(vw_ref[0, pl.ds(r0, LANES), :]).astype(_CD)
            return c
        lax.fori_loop(0, S // TK, tr, 0)
        for n in range(ncp // LANES):
            cols = slice(n * LANES, (n + 1) * LANES)
            vct_ref[:, cols] = _t(vc_ref[0, 0, 0, cols, :]).astype(_CD)

    q = q_ref[0]
    qt = jnp.concatenate([_t(q[:, h * HEAD_DIM:(h + 1) * HEAD_DIM]).astype(_CD) for h in range(H)], axis=1)
    pos_l = t0 + (lax.broadcasted_iota(jnp.int32, (1, R), 1) & (TQ - 1))

    gt_ref[...] = jnp.transpose(jax.nn.sigmoid(gate_ref[0]))

    def gate_row(c):
        return jnp.concatenate([gt_ref[pl.ds(gate_lane0 + (g * H + h) * 3 + c, 1), :] for h in range(H)], axis=1)

    s_c = _dot(kc_ref[0, 0, 0], qt) * SCALE
    cmp_end = lax.broadcasted_iota(jnp.int32, (ncp, 1), 0) * NSA_CMP_STRIDE + (NSA_CMP_LEN - 1)
    valid = cmp_end <= pos_l
    s_c = jnp.where(valid, s_c, NEG_INF)
    p = jnp.where(valid, jnp.exp(s_c - jnp.max(s_c, axis=0, keepdims=True)), 0.0)
    lsum = jnp.sum(p, axis=0, keepdims=True)
    p_c = p / jnp.where(lsum > 0.0, lsum, 1.0)
    out_ref[...] = gate_row(0) * _dot(vct_ref[...], p_c.astype(_CD))

    p_g = p_c[:, 0:TQ]
    for h in range(1, H):
        p_g = p_g + p_c[:, h * TQ:(h + 1) * TQ]
    p_hi, p_lo = _split(p_g)
    ovt = ovt_ref[...]
    imp = _dot(ovt, p_hi) + _dot(ovt, p_lo)
    j_io = lax.broadcasted_iota(jnp.int32, (SEL_PAD, TQ), 0)
    cur = (t0 + lax.broadcasted_iota(jnp.int32, (SEL_PAD, TQ), 1)) // NSA_SEL_BLOCK
    forced = (j_io == 0) | (j_io == cur) | (j_io == cur - 1)
    imp = jnp.where(j_io <= cur, jnp.where(forced, FORCE_SCORE, imp), NEG_INF)
    imp_ref[...] = imp

    def rank_step(mm, cnt):
        for u in range(RANK_UNROLL):
            m = mm * RANK_UNROLL + u
            row = imp_ref[pl.ds(m, 1), :]
            ahead = (row > imp) | ((row == imp) & (j_io > m))
            cnt = cnt + jnp.where(ahead, 1.0, 0.0)
        return cnt

    n_cand = (t0 + TQ - 1) // NSA_SEL_BLOCK + 1
    rank = lax.fori_loop(0, (n_cand + RANK_UNROLL - 1) // RANK_UNROLL, rank_step, jnp.zeros((SEL_PAD, TQ), _F32))
    sel_bias = jnp.where(rank < float(NSA_TOP_N), 0.0, NEG_INF)
    selb_ref[...] = jnp.concatenate([sel_bias] * H, axis=1)

    key_io = lax.broadcasted_iota(jnp.int32, (TK, 1), 0)
    init = (jnp.full((1, R), NEG_INF, _F32), jnp.zeros((1, R), _F32))
    spb = TK // NSA_SEL_BLOCK
    n_full = t0 // TK

    def sel_scores(kb):
        k0 = pl.multiple_of(kb * TK, TK)
        s2 = _dot(ks_ref[0, pl.ds(k0, TK), :], qt) * SCALE_LOG2
        return s2 + _bias_rows(selb_ref, kb * spb, spb, NSA_SEL_BLOCK, R)

    acc_ref[...] = jnp.zeros(acc_ref.shape, _F32)
    causal = jnp.where((n_full * TK + key_io) <= pos_l, 0.0, NEG_INF)
    carry = _online_step_t(sel_scores(n_full) + causal, vst_ref[n_full], *init, acc_ref)
    _, l_s = lax.fori_loop(0, n_full, lambda kb, c: _online_step_t(sel_scores(kb), vst_ref[kb], *c, acc_ref), carry)
    out_ref[...] += gate_row(1) * (acc_ref[...] / l_s)

    wk = NSA_WINDOW + TQ
    w0 = pl.multiple_of(jnp.maximum(t0 - NSA_WINDOW, 0), TQ)
    dist = pos_l - (w0 + lax.broadcasted_iota(jnp.int32, (wk, 1), 0))
    in_win = (dist >= 0) & (dist < NSA_WINDOW)
    s_w = jnp.where(in_win, _dot(kw_ref[0, pl.ds(w0, wk), :], qt) * SCALE_LOG2, NEG_INF)
    p_w = jnp.exp2(s_w - jnp.max(s_w, axis=0, keepdims=True))
    l_w = jnp.sum(p_w, axis=0, keepdims=True)
    p_w = p_w.astype(_CD)
    acc_w = _dot(vwt_ref[w0 // LANES], p_w[0:LANES])
    for u in range(1, wk // LANES):
        acc_w = acc_w + _dot(vwt_ref[w0 // LANES + u], p_w[u * LANES:(u + 1) * LANES])
    out_t = out_ref[...] + gate_row(2) * (acc_w / l_w)

    for h in range(H):
        o_ref[0, :, h * HEAD_DIM:(h + 1) * HEAD_DIM] = jnp.transpose(out_t[:, h * TQ:(h + 1) * TQ]).astype(o_ref.dtype)


def _overlap_t(S):
    ncp = S // NSA_CMP_STRIDE
    nc = (S - NSA_CMP_LEN) // NSA_CMP_STRIDE + 1
    ns = S // NSA_SEL_BLOCK
    cs = np.arange(nc)[:, None] * NSA_CMP_STRIDE
    ss = np.arange(ns)[None, :] * NSA_SEL_BLOCK
    ov = np.clip(np.minimum(cs + NSA_CMP_LEN, ss + NSA_SEL_BLOCK) - np.maximum(cs, ss), 0, None) / NSA_CMP_LEN
    out = np.zeros((SEL_PAD, ncp), np.float32)
    out[:ns, :nc] = ov.T
    return out


def _nsa_attention(zb3, kvc, zr3, gate_col, S):
    B = zb3.shape[0]
    G, TQ, TK = NSA_KV_HEADS, NSA_TQ, NSA_TK
    assert S // NSA_SEL_BLOCK <= SEL_PAD and S % TK == 0 and TK % TQ == 0 and S >= NSA_WINDOW + TQ
    ncp = S // NSA_CMP_STRIDE
    assert ncp % LANES == 0
    qw = NSA_HPG * HEAD_DIM
    R = NSA_HPG * TQ
    ovt = jnp.asarray(_overlap_t(S)).astype(_CD)
    resident = lambda col: pl.BlockSpec((1, S, HEAD_DIM), lambda b, g, i: (b, 0, col // HEAD_DIM + g))
    return pl.pallas_call(
        functools.partial(_nsa_body, S=S, gate_lane0=gate_col % LANES),
        grid=(B, G, S // TQ),
        in_specs=[pl.BlockSpec((1, TQ, qw), lambda b, g, i: (b, i, g)),
                  resident(COL_KS), resident(COL_VS), resident(COL_KW), resident(COL_VW),
                  pl.BlockSpec((1, 1, 1, ncp, HEAD_DIM), lambda b, g, i: (b, 0, g, 0, 0)),
                  pl.BlockSpec((1, 1, 1, ncp, HEAD_DIM), lambda b, g, i: (b, 1, g, 0, 0)),
                  pl.BlockSpec((1, TQ, LANES), lambda b, g, i: (b, i, gate_col // LANES)),
                  pl.BlockSpec((SEL_PAD, ncp), lambda b, g, i: (0, 0))],
        out_specs=pl.BlockSpec((1, TQ, qw), lambda b, g, i: (b, i, g)),
        out_shape=jax.ShapeDtypeStruct((B, S, NSA_Q_W), _CD),
        scratch_shapes=[pltpu.VMEM((S // TK, HEAD_DIM, TK), _CD),
                        pltpu.VMEM((S // LANES, HEAD_DIM, LANES), _CD),
                        pltpu.VMEM((HEAD_DIM, ncp), _CD),
                        pltpu.VMEM((SEL_PAD, TQ), _F32),
                        pltpu.VMEM((SEL_PAD, R), _F32),
                        pltpu.VMEM((LANES, TQ), _F32),
                        pltpu.VMEM((HEAD_DIM, R), _F32),
                        pltpu.VMEM((HEAD_DIM, R), _F32)],
        compiler_params=_params(("parallel", "parallel", "arbitrary")),
        name="nsa_attention",
    )(zb3, zb3, zb3, zb3, zb3, kvc, kvc, zr3, ovt)


MOBA_NB = 4


def _moba_body(q_ref, k_ref, v_ref, o_ref, kmean_ref, gate_ref, selb_ref, vt_ref, acc_ref, *, S):
    TQ, NB = MOBA_BLOCK, MOBA_NB
    TK = NB * MOBA_BLOCK
    nf = S // MOBA_BLOCK
    nfp = kmean_ref.shape[0]
    qi = pl.program_id(2)
    t0 = pl.multiple_of(qi * TQ, TQ)

    @pl.when(qi == 0)
    def _():
        kmean_ref[...] = jnp.zeros(kmean_ref.shape, _F32)
        for n in range(nf):
            rows = slice(n * MOBA_BLOCK, (n + 1) * MOBA_BLOCK)
            kmean_ref[n:n + 1, :] = jnp.mean(k_ref[0, rows, :].astype(_F32), axis=0, keepdims=True)
            cols = slice((n % NB) * MOBA_BLOCK, (n % NB + 1) * MOBA_BLOCK)
            vt_ref[n // NB, :, cols] = _t(v_ref[0, rows, :]).astype(_CD)

    qt = _t(q_ref[0]).astype(_CD)
    km_hi, km_lo = _split(kmean_ref[...])
    gate = _dot(km_hi, qt) + _dot(km_lo, qt)
    n_io = lax.broadcasted_iota(jnp.int32, (nfp, TQ), 0)
    gate = jnp.where(n_io < qi, gate, NEG_INF)
    gate_ref[...] = gate

    def rank_step(mm, cnt):
        for u in range(RANK_UNROLL):
            m = mm * RANK_UNROLL + u
            row = gate_ref[pl.ds(m, 1), :]
            ahead = (row > gate) | ((row == gate) & (n_io > m))
            cnt = cnt + jnp.where(ahead, 1.0, 0.0)
        return cnt

    rank = lax.fori_loop(0, (qi + RANK_UNROLL - 1) // RANK_UNROLL, rank_step, jnp.zeros((nfp, TQ), _F32))
    keep = ((rank < float(MOBA_TOPK)) & (n_io < qi)) | (n_io == qi)
    selb_ref[...] = jnp.where(keep, 0.0, NEG_INF)

    pos_l = t0 + lax.broadcasted_iota(jnp.int32, (1, TQ), 1)
    key_io = lax.broadcasted_iota(jnp.int32, (TK, 1), 0)
    init = (jnp.full((1, TQ), NEG_INF, _F32), jnp.zeros((1, TQ), _F32))
    n_full = qi // NB

    def scores(kb):
        k0 = pl.multiple_of(kb * TK, TK)
        s2 = _dot(k_ref[0, pl.ds(k0, TK), :], qt) * SCALE_LOG2
        return s2 + _bias_rows(selb_ref, kb * NB, NB, MOBA_BLOCK, TQ)

    acc_ref[...] = jnp.zeros(acc_ref.shape, _F32)
    causal = jnp.where((n_full * TK + key_io) <= pos_l, 0.0, NEG_INF)
    carry = _online_step_t(scores(n_full) + causal, vt_ref[n_full], *init, acc_ref)
    _, l = lax.fori_loop(0, n_full, lambda kb, c: _online_step_t(scores(kb), vt_ref[kb], *c, acc_ref), carry)
    o_ref[0] = jnp.transpose(acc_ref[...] / l).astype(o_ref.dtype)


def _moba_attention(zb3, S):
    B = zb3.shape[0]
    H, TQ = MOBA_HEADS, MOBA_BLOCK
    assert S % MOBA_BLOCK == 0
    nf = S // MOBA_BLOCK
    nfp = -(-nf // 8) * 8
    assert nfp % RANK_UNROLL == 0 and nf % MOBA_NB == 0
    tile = lambda col: pl.BlockSpec((1, TQ, HEAD_DIM), lambda b, h, i: (b, i, col // HEAD_DIM + h))
    resident = lambda col: pl.BlockSpec((1, S, HEAD_DIM), lambda b, h, i: (b, 0, col // HEAD_DIM + h))
    return pl.pallas_call(
        functools.partial(_moba_body, S=S),
        grid=(B, H, S // TQ),
        in_specs=[tile(COL_QB), resident(COL_KB), resident(COL_VB)],
        out_specs=pl.BlockSpec((1, TQ, HEAD_DIM), lambda b, h, i: (b, i, h)),
        out_shape=jax.ShapeDtypeStruct((B, S, MOBA_W), _CD),
        scratch_shapes=[pltpu.VMEM((nfp, HEAD_DIM), _F32),
                        pltpu.VMEM((nfp, TQ), _F32),
                        pltpu.VMEM((nfp, TQ), _F32),
                        pltpu.VMEM((nf // MOBA_NB, HEAD_DIM, MOBA_NB * MOBA_BLOCK), _CD),
                        pltpu.VMEM((HEAD_DIM, TQ), _F32)],
        compiler_params=_params(("parallel", "parallel", "arbitrary")),
        name="moba_attention",
    )(zb3, zb3, zb3)


def _router_body(h_ref, g_ref, wr_ref, f_ref, eid_ref, wts_ref):
    x = h_ref[...]
    y = x * lax.rsqrt(jnp.mean(x * x, axis=-1, keepdims=True) + NORM_EPS)
    f = y * g_ref[...]
    f_ref[...] = f
    f_hi, f_lo = _split(f)
    w_hi, w_lo = _split(wr_ref[...])
    logits = _dot(f_hi, w_hi) + (_dot(f_hi, w_lo) + _dot(f_lo, w_hi))
    lane = lax.broadcasted_iota(jnp.int32, logits.shape, 1)
    big = jnp.int32(2 * LANES)
    in_g = lane < MOE_GROUPS
    lg = jnp.where(in_g, logits, -jnp.inf)
    mg = jnp.max(lg, axis=1, keepdims=True)
    pg_top = 1.0 / jnp.sum(jnp.where(in_g, jnp.exp(logits - mg), 0.0), axis=1, keepdims=True)
    grp = jnp.min(jnp.where(lg == mg, lane, big), axis=1, keepdims=True)
    eidx = lane - MOE_GROUPS
    in_e = (eidx >= 0) & (eidx < MOE_EXPERTS) & ((eidx // MOE_EXPERTS_PER_GROUP) == grp)
    le = jnp.where(in_e, logits, -jnp.inf)
    m1 = jnp.max(le, axis=1, keepdims=True)
    i1 = jnp.min(jnp.where(le == m1, lane, big), axis=1, keepdims=True)
    le2 = jnp.where(lane == i1, -jnp.inf, le)
    m2 = jnp.max(le2, axis=1, keepdims=True)
    i2 = jnp.min(jnp.where(le2 == m2, lane, big), axis=1, keepdims=True)
    e2 = jnp.exp(m2 - m1)
    w1 = pg_top / (1.0 + e2)
    w2 = pg_top * e2 / (1.0 + e2)
    eid_ref[...] = jnp.where(lane == 0, i1 - MOE_GROUPS, jnp.where(lane == 1, i2 - MOE_GROUPS, 0))
    wts_ref[...] = jnp.where(lane == 0, w1, jnp.where(lane == 1, w2, 0.0))


def _norm_router(h, g, wr, tm=256):
    T, D = h.shape
    row = lambda w: pl.BlockSpec((tm, w), lambda i: (i, 0))
    return pl.pallas_call(
        _router_body,
        grid=(T // tm,),
        in_specs=[row(D), pl.BlockSpec((1, D), lambda i: (0, 0)), pl.BlockSpec((D, LANES), lambda i: (0, 0))],
        out_specs=[row(D), row(LANES), row(LANES)],
        out_shape=[jax.ShapeDtypeStruct((T, D), _F32), jax.ShapeDtypeStruct((T, LANES), jnp.int32),
                   jax.ShapeDtypeStruct((T, LANES), _F32)],
        compiler_params=_params(("parallel",)),
        name="ffn_norm_router",
    )(h, g.reshape(1, D), wr)


MOE_ROWS = 256


def _expert_body(blk_e_ref, nused_ref, gsrc_ref, sdst_ref, f_hbm, w_ref, wg_ref, wu_ref, wd_ref, y_hbm,
                 xbuf, obuf, sem_in, sem_out):
    i = pl.program_id(0)
    nused = nused_ref[0]
    slot = i % 2
    other = 1 - slot

    def gather(blk, s, r):
        return pltpu.make_async_copy(f_hbm.at[pl.ds(gsrc_ref[blk * MOE_ROWS + r], 1)], xbuf.at[s, pl.ds(r, 1)],
                                     sem_in.at[s])

    def scatter(blk1, s, r):
        return pltpu.make_async_copy(obuf.at[s, pl.ds(r, 1)], y_hbm.at[pl.ds(sdst_ref[blk1 * MOE_ROWS + r], 1)],
                                     sem_out.at[s])

    def wait_gather(s):
        pltpu.make_async_copy(f_hbm.at[pl.ds(0, MOE_ROWS)], xbuf.at[s], sem_in.at[s]).wait()

    def wait_scatter(s):
        pltpu.make_async_copy(obuf.at[s], y_hbm.at[pl.ds(0, MOE_ROWS)], sem_out.at[s]).wait()

    @pl.when(i == 0)
    def _():
        obuf[...] = jnp.zeros(obuf.shape, _F32)
        spare0 = pltpu.make_async_copy(obuf.at[0], y_hbm.at[pl.ds(y_hbm.shape[0] - 2 * MOE_ROWS, MOE_ROWS)],
                                       sem_out.at[0])
        spare0.start()
        spare0.wait()

        def start(r, c):
            gather(0, 0, r).start()
            return c
        lax.fori_loop(0, MOE_ROWS, start, 0)

    @pl.when(i <= nused)
    def _():
        wait_gather(slot)

        @pl.when(i >= 1)
        def _():
            wait_scatter(slot)

        for r in range(MOE_ROWS):
            gather(i + 1, other, r).start()
            scatter(i, other, r).start()

        x = xbuf[slot].astype(_CD)
        hb = (jax.nn.silu(_dot(x, wg_ref[0])) * _dot(x, wu_ref[0])).astype(_CD)
        obuf[slot] = _dot(hb, wd_ref[0]) * w_ref[...]

        @pl.when(i == nused)
        def _():
            wait_scatter(other)
            wait_gather(other)


def _experts(f, blk_e, nused, gsrc, sdst, buf_w, wg, wu, wd):
    T, D = f.shape
    E, _, FF = wg.shape
    n_blocks = blk_e.shape[0]
    once = pl.Buffered(1)
    grid_spec = pltpu.PrefetchScalarGridSpec(
        num_scalar_prefetch=4,
        grid=(n_blocks,),
        in_specs=[pl.BlockSpec(memory_space=pl.ANY),
                  pl.BlockSpec((MOE_ROWS, 1), lambda i, be, nu, gs, sd: (i, 0)),
                  pl.BlockSpec((1, D, FF), lambda i, be, nu, gs, sd: (be[i], 0, 0), pipeline_mode=once),
                  pl.BlockSpec((1, D, FF), lambda i, be, nu, gs, sd: (be[i], 0, 0), pipeline_mode=once),
                  pl.BlockSpec((1, FF, D), lambda i, be, nu, gs, sd: (be[i], 0, 0), pipeline_mode=once)],
        out_specs=pl.BlockSpec(memory_space=pl.ANY),
        scratch_shapes=[pltpu.VMEM((2, MOE_ROWS, D), _F32), pltpu.VMEM((2, MOE_ROWS, D), _F32),
                        pltpu.SemaphoreType.DMA((2,)), pltpu.SemaphoreType.DMA((2,))],
    )
    return pl.pallas_call(
        _expert_body,
        grid_spec=grid_spec,
        out_shape=jax.ShapeDtypeStruct((MOE_TOPK * T + 2 * MOE_ROWS, D), _F32),
        compiler_params=_params(("arbitrary",), VMEM_LIMIT_EXPERTS),
        name="moe_experts",
    )(blk_e, nused, gsrc, sdst, f, buf_w, wg, wu, wd)


def _group_by_expert(eid, wts):
    T = eid.shape[0]
    n_assign = T * MOE_TOPK
    n_blocks = -(-n_assign // MOE_ROWS) + MOE_EXPERTS
    e_flat = eid.reshape(-1)
    order = jnp.argsort(e_flat, stable=True).astype(jnp.int32)
    counts = jnp.bincount(e_flat, length=MOE_EXPERTS).astype(jnp.int32)
    starts = jnp.cumsum(counts) - counts
    pcounts = (counts + MOE_ROWS - 1) // MOE_ROWS * MOE_ROWS
    pends = jnp.cumsum(pcounts)
    pstarts = pends - pcounts
    blk_start = jnp.arange(n_blocks, dtype=jnp.int32) * MOE_ROWS
    blk_e = jnp.minimum(jnp.sum(pends[None, :] <= blk_start[:, None], axis=1), MOE_EXPERTS - 1).astype(jnp.int32)
    slot = jnp.arange(n_blocks * MOE_ROWS, dtype=jnp.int32)
    e_p = jnp.repeat(blk_e, MOE_ROWS)
    off = slot - pstarts[e_p]
    valid = (off >= 0) & (off < counts[e_p])
    src = order[jnp.clip(starts[e_p] + off, 0, n_assign - 1)]
    buf_w = jnp.where(valid, wts.reshape(-1)[src], 0.0)
    spare = n_assign + ((slot // MOE_ROWS) % 2) * MOE_ROWS + slot % MOE_ROWS
    gsrc = jnp.where(valid, src // MOE_TOPK, 0)
    sdst = jnp.where(valid, (src % MOE_TOPK) * T + src // MOE_TOPK, spare)
    gsrc = jnp.concatenate([gsrc, jnp.zeros((MOE_ROWS,), jnp.int32)]).astype(jnp.int32)
    sdst = jnp.concatenate([n_assign + MOE_ROWS + jnp.arange(MOE_ROWS, dtype=jnp.int32), sdst]).astype(jnp.int32)
    nused = (pends[-1:] // MOE_ROWS).astype(jnp.int32)
    return blk_e, nused, gsrc, sdst, buf_w.reshape(-1, 1)


def _addnorm_body(h_ref, y0_ref, y1_ref, g_ref, h2_ref, n_ref):
    h2 = h_ref[...] + (y0_ref[...] + y1_ref[...])
    h2_ref[...] = h2
    y = h2 * lax.rsqrt(jnp.mean(h2 * h2, axis=-1, keepdims=True) + NORM_EPS)
    n_ref[...] = (y * g_ref[...]).astype(n_ref.dtype)


def _add_norm(h, y2, g, tm=256):
    T, D = h.shape
    nt = T // tm
    return pl.pallas_call(
        _addnorm_body,
        grid=(nt,),
        in_specs=[pl.BlockSpec((tm, D), lambda i: (i, 0)),
                  pl.BlockSpec((tm, D), lambda i: (i, 0)),
                  pl.BlockSpec((tm, D), lambda i: (nt + i, 0)),
                  pl.BlockSpec((1, D), lambda i: (0, 0))],
        out_specs=[pl.BlockSpec((tm, D), lambda i: (i, 0)), pl.BlockSpec((tm, D), lambda i: (i, 0))],
        out_shape=[jax.ShapeDtypeStruct((T, D), _F32), jax.ShapeDtypeStruct((T, D), _CD)],
        compiler_params=_params(("parallel",)),
        name="moe_add_norm",
    )(h, y2, y2, g.reshape(1, D))


def kernel(x, p, attn_norm, w_in, nsa_cmp_pe, nsa_cmp_w1, nsa_cmp_w2, w_up_nsa, w_up_moba, w_out, ffn_norm, router_group, router_expert, expert_w_gate, expert_w_up, expert_w_down, ple_norm, ple_gate, ple_proj, final_norm):
    B, S, D = x.shape
    depth = w_in.shape[0]
    T = B * S
    tm = min(512, T)
    tm_in = min(1024, S)
    tn = min(512, D)
    assert D % tn == 0 and T % tm == 0 and S % tm_in == 0

    o_kv = NSA_Q_W
    o_gate = o_kv + 6 * NSA_KV_W
    o_qkvb = o_gate + NSA_GATE_W
    o_merge = o_qkvb + 3 * MOBA_W
    col_gate = COL_MERGE + 2 * D
    n_rest = col_gate + GATE_PAD

    cos, sin = _rope_tables(jnp.arange(S))
    ccos, csin = _rope_tables(jnp.arange(S // NSA_CMP_STRIDE) * NSA_CMP_STRIDE + NSA_CMP_LEN - 1)

    h = x.reshape(T, D)
    for i in range(depth):
        wi = w_in[i]
        w_heads = jnp.concatenate([wi[:, 0:o_kv], wi[:, o_kv + 2 * NSA_KV_W:o_gate], wi[:, o_qkvb:o_merge]],
                                  axis=1).astype(_CD)
        w_rest = jnp.concatenate([wi[:, o_kv:o_kv + 2 * NSA_KV_W], wi[:, o_merge:o_merge + 2 * D],
                                  wi[:, o_gate:o_qkvb], jnp.zeros((D, GATE_PAD - NSA_GATE_W), wi.dtype)],
                                 axis=1).astype(_CD)
        a = _rmsnorm(h, attn_norm[i], _CD)
        zb3 = _head_proj(a, w_heads, cos, sin, S, tm_in).reshape(B, S, HEADS_W)
        zr = _mm_call(_mm_plain_body, [a], [w_rest], [], n_rest, _F32, tm_in, tn, "in_proj_rest")
        zr3 = zr.reshape(B, S, n_rest)
        kvc = _nsa_compress(zr3, nsa_cmp_pe[i], nsa_cmp_w1[i], nsa_cmp_w2[i], ccos, csin)
        o_a = _nsa_attention(zb3, kvc, zr3, col_gate, S).reshape(T, NSA_Q_W)
        o_b = _moba_attention(zb3, S).reshape(T, MOBA_W)
        mixed = _mm_call(_mm_gated_body, [o_a, o_b], [w_up_nsa[i].astype(_CD), w_up_moba[i].astype(_CD)],
                         [(zr, COL_MERGE), (zr, COL_MERGE + D)], D, _CD, tm, tn, "merge_up")
        h = _mm_call(_mm_resid_body, [mixed], [w_out[i].astype(_CD)], [(h, 0)], D, _F32, tm, tn, "out_proj")

        wr = jnp.concatenate([router_group[i], router_expert[i],
                              jnp.zeros((D, LANES - MOE_GROUPS - MOE_EXPERTS), _F32)], axis=1)
        f, eid, wts = _norm_router(h, ffn_norm[i], wr)
        blk_e, nused, gsrc, sdst, buf_w = _group_by_expert(eid[:, :MOE_TOPK], wts[:, :MOE_TOPK])
        y2 = _experts(f, blk_e, nused, gsrc, sdst, buf_w, expert_w_gate[i].astype(_CD),
                      expert_w_up[i].astype(_CD), expert_w_down[i].astype(_CD))
        h2, n = _add_norm(h, y2, ple_norm[i])
        h = _mm_call(_mm_ple_body, [n, p[i].reshape(T, -1)], [ple_gate[i].astype(_CD), ple_proj[i].astype(_CD)],
                     [(h2, 0)], D, _F32, tm, tn, "ple")
    return _rmsnorm(h, final_norm, _F32).reshape(B, S, D)
```

```python
import functools

import numpy as np
import jax
import jax.numpy as jnp
from jax import lax
from jax.experimental import pallas as pl
from jax.experimental.pallas import tpu as pltpu

HEAD_DIM = 128
ROPE_THETA = 10000.0
NORM_EPS = 1e-6
NEG_INF = -1e30
FORCE_SCORE = 1e9
SCALE = HEAD_DIM ** -0.5

NSA_Q_HEADS = 16
NSA_KV_HEADS = 4
NSA_HPG = NSA_Q_HEADS // NSA_KV_HEADS
NSA_CMP_LEN = 32
NSA_CMP_STRIDE = 16
NSA_SEL_BLOCK = 64
NSA_TOP_N = 16
NSA_WINDOW = 512
NSA_Q_W = NSA_Q_HEADS * HEAD_DIM
NSA_KV_W = NSA_KV_HEADS * HEAD_DIM
NSA_GATE_W = NSA_Q_HEADS * 3

MOBA_HEADS = 16
MOBA_BLOCK = 256
MOBA_TOPK = 3
MOBA_W = MOBA_HEADS * HEAD_DIM

MOE_GROUPS = 4
MOE_EXPERTS_PER_GROUP = 8
MOE_EXPERTS = MOE_GROUPS * MOE_EXPERTS_PER_GROUP
MOE_TOPK = 2

LANES = 128
SEL_PAD = 128
GATE_PAD = 512
VMEM_LIMIT = 48 * 1024 * 1024
VMEM_LIMIT_EXPERTS = 56 * 1024 * 1024

COL_QA = 0
COL_KS = COL_QA + NSA_Q_W
COL_VS = COL_KS + NSA_KV_W
COL_KW = COL_VS + NSA_KV_W
COL_VW = COL_KW + NSA_KV_W
COL_QB = COL_VW + NSA_KV_W
COL_KB = COL_QB + MOBA_W
COL_VB = COL_KB + MOBA_W
HEADS_W = COL_VB + MOBA_W
COL_KC = 0
COL_VC = COL_KC + NSA_KV_W
COL_MERGE = COL_VC + NSA_KV_W

_CD = jnp.bfloat16
_F32 = jnp.float32


def _dot(a, b):
    return jnp.dot(a, b, preferred_element_type=_F32)


def _split(x):
    hi = x.astype(_CD)
    lo = (x - hi.astype(_F32)).astype(_CD)
    return hi, lo


def _t(x):
    return jnp.transpose(x.astype(_F32))


def _params(sem, vmem=VMEM_LIMIT):
    return pltpu.CompilerParams(dimension_semantics=sem, vmem_limit_bytes=vmem)


def _rmsnorm_body(x_ref, g_ref, o_ref):
    x = x_ref[...]
    y = x * lax.rsqrt(jnp.mean(x * x, axis=-1, keepdims=True) + NORM_EPS)
    o_ref[...] = (y * g_ref[...]).astype(o_ref.dtype)


def _rmsnorm(x, g, out_dtype, tm=256):
    T, D = x.shape
    return pl.pallas_call(
        _rmsnorm_body,
        grid=(T // tm,),
        in_specs=[pl.BlockSpec((tm, D), lambda i: (i, 0)),
                  pl.BlockSpec((1, D), lambda i: (0, 0))],
        out_specs=pl.BlockSpec((tm, D), lambda i: (i, 0)),
        out_shape=jax.ShapeDtypeStruct((T, D), out_dtype),
        compiler_params=_params(("parallel",)),
        name="rmsnorm",
    )(x, g.reshape(1, D))


def _mm_call(body, lhs, rhs, extras, n_out, out_dtype, tm, tn, name):
    M = lhs[0].shape[0]
    in_specs = [pl.BlockSpec((tm, a.shape[1]), lambda i, j: (i, 0)) for a in lhs]
    in_specs += [pl.BlockSpec((w.shape[0], tn), lambda i, j: (0, j)) for w in rhs]
    args = list(lhs) + list(rhs)
    for arr, off in extras:
        assert off % tn == 0
        in_specs.append(pl.BlockSpec((tm, tn), functools.partial(lambda i, j, o: (i, j + o), o=off // tn)))
        args.append(arr)
    return pl.pallas_call(
        body,
        grid=(M // tm, n_out // tn),
        in_specs=in_specs,
        out_specs=pl.BlockSpec((tm, tn), lambda i, j: (i, j)),
        out_shape=jax.ShapeDtypeStruct((M, n_out), out_dtype),
        compiler_params=_params(("parallel", "parallel")),
        name=name,
    )(*args)


def _mm_plain_body(a_ref, w_ref, o_ref):
    o_ref[...] = _dot(a_ref[...], w_ref[...]).astype(o_ref.dtype)


def _mm_gated_body(a1_ref, a2_ref, w1_ref, w2_ref, m1_ref, m2_ref, o_ref):
    y1 = _dot(a1_ref[...], w1_ref[...])
    y2 = _dot(a2_ref[...], w2_ref[...])
    o_ref[...] = (jax.nn.sigmoid(m1_ref[...]) * y1 + jax.nn.sigmoid(m2_ref[...]) * y2).astype(o_ref.dtype)


def _mm_resid_body(a_ref, w_ref, r_ref, o_ref):
    o_ref[...] = r_ref[...] + _dot(a_ref[...], w_ref[...])


def _mm_ple_body(n_ref, p_ref, wg_ref, wp_ref, r_ref, o_ref):
    gate = jax.nn.sigmoid(_dot(n_ref[...], wg_ref[...]))
    emb = _dot(p_ref[...].astype(_CD), wp_ref[...])
    o_ref[...] = r_ref[...] + gate * emb


def _rope_tables(pos):
    half = HEAD_DIM // 2
    inv = jnp.power(ROPE_THETA, -jnp.arange(half, dtype=_F32) / half)
    ang = pos.astype(_F32)[:, None] * inv
    c, s = jnp.cos(ang), jnp.sin(ang)
    return jnp.concatenate([c, c], axis=-1), jnp.concatenate([-s, s], axis=-1)


def _rope(x, c, s):
    return x * c + pltpu.roll(x, HEAD_DIM // 2, 1) * s


HEADS_TN = 512
_ROPE_BLOCKS = ((COL_QA // HEADS_TN, COL_KS // HEADS_TN), (COL_KS // HEADS_TN, COL_VS // HEADS_TN),
                (COL_KW // HEADS_TN, COL_VW // HEADS_TN), (COL_QB // HEADS_TN, COL_VB // HEADS_TN))


def _mm_heads_body(a_ref, w_ref, cos_ref, sin_ref, o_ref):
    j = pl.program_id(1)
    y = _dot(a_ref[...], w_ref[...])
    is_rope = functools.reduce(jnp.logical_or, [(j >= a) & (j < b) for a, b in _ROPE_BLOCKS])

    @pl.when(is_rope)
    def _():
        c = cos_ref[...]
        s = sin_ref[...]
        for h in range(HEADS_TN // HEAD_DIM):
            sl = slice(h * HEAD_DIM, (h + 1) * HEAD_DIM)
            o_ref[:, sl] = _rope(y[:, sl], c, s).astype(o_ref.dtype)

    @pl.when(jnp.logical_not(is_rope))
    def _():
        o_ref[...] = y.astype(o_ref.dtype)


def _head_proj(a, w, cos, sin, S, tm):
    T, D = a.shape
    nps = S // tm
    return pl.pallas_call(
        _mm_heads_body,
        grid=(T // tm, HEADS_W // HEADS_TN),
        in_specs=[pl.BlockSpec((tm, D), lambda i, j: (i, 0)),
                  pl.BlockSpec((D, HEADS_TN), lambda i, j: (0, j)),
                  pl.BlockSpec((tm, HEAD_DIM), lambda i, j: (i % nps, 0)),
                  pl.BlockSpec((tm, HEAD_DIM), lambda i, j: (i % nps, 0))],
        out_specs=pl.BlockSpec((tm, HEADS_TN), lambda i, j: (i, j)),
        out_shape=jax.ShapeDtypeStruct((T, HEADS_W), _CD),
        compiler_params=_params(("parallel", "parallel")),
        name="in_proj_heads",
    )(a, w, cos, sin)


def _compress_body(x_ref, pe_ref, w1_ref, w2_ref, cos_ref, sin_ref, o_ref, xpad_ref, *, S, nc):
    ncp = S // NSA_CMP_STRIDE
    xpad_ref[0:S, :] = x_ref[0]
    xpad_ref[S:S + NSA_CMP_LEN, :] = jnp.zeros((NSA_CMP_LEN, HEAD_DIM), _F32)
    acc = jnp.zeros((ncp, HEAD_DIM), _F32)
    for l in range(NSA_CMP_LEN):
        xl = xpad_ref[pl.ds(l, ncp, stride=NSA_CMP_STRIDE), :] + pe_ref[0, l:l + 1, :]
        acc = acc + _dot(xl.astype(_CD), w1_ref[0, l].astype(_CD))
    hid = jax.nn.gelu(acc)
    out = _dot(hid.astype(_CD), w2_ref[0].astype(_CD))
    is_key = pl.program_id(1) == 0
    out = jnp.where(is_key, _rope(out, cos_ref[...], sin_ref[...]), out)
    row = lax.broadcasted_iota(jnp.int32, (ncp, HEAD_DIM), 0)
    o_ref[0, 0, 0] = jnp.where(row < nc, out, 0.0).astype(o_ref.dtype)


def _nsa_compress(zr3, pe, w1, w2, ccos, csin):
    B, S, _ = zr3.shape
    G = NSA_KV_HEADS
    ncp = S // NSA_CMP_STRIDE
    nc = (S - NSA_CMP_LEN) // NSA_CMP_STRIDE + 1
    cb = COL_KC // HEAD_DIM
    return pl.pallas_call(
        functools.partial(_compress_body, S=S, nc=nc),
        grid=(B, 2, G),
        in_specs=[pl.BlockSpec((1, S, HEAD_DIM), lambda b, t, g: (b, 0, cb + t * G + g)),
                  pl.BlockSpec((1, NSA_CMP_LEN, HEAD_DIM), lambda b, t, g: (t, 0, 0)),
                  pl.BlockSpec((1, NSA_CMP_LEN, HEAD_DIM, HEAD_DIM), lambda b, t, g: (t, 0, 0, 0)),
                  pl.BlockSpec((1, HEAD_DIM, HEAD_DIM), lambda b, t, g: (t, 0, 0)),
                  pl.BlockSpec((ncp, HEAD_DIM), lambda b, t, g: (0, 0)),
                  pl.BlockSpec((ncp, HEAD_DIM), lambda b, t, g: (0, 0))],
        out_specs=pl.BlockSpec((1, 1, 1, ncp, HEAD_DIM), lambda b, t, g: (b, t, g, 0, 0)),
        out_shape=jax.ShapeDtypeStruct((B, 2, G, ncp, HEAD_DIM), _CD),
        scratch_shapes=[pltpu.VMEM((S + NSA_CMP_LEN, HEAD_DIM), _F32)],
        compiler_params=_params(("parallel", "parallel", "parallel")),
        name="nsa_compress",
    )(zr3, pe, w1, w2, ccos, csin)


SCALE_LOG2 = SCALE * 1.4426950408889634


def _online_step_t(s2, vt, m, l, acc_ref, allowed=None):
    m_new = jnp.maximum(m, jnp.max(s2, axis=0, keepdims=True))
    alpha = jnp.exp2(m - m_new)
    p = jnp.exp2(s2 - m_new)
    if allowed is not None:
        p = jnp.where(allowed, p, 0.0)
    l = alpha * l + jnp.sum(p, axis=0, keepdims=True)
    acc_ref[...] = alpha * acc_ref[...] + _dot(vt, p.astype(_CD))
    return m_new, l


def _bias_rows(ref, first, count, rows_each, width):
    return jnp.concatenate([jnp.broadcast_to(ref[pl.ds(first + u, 1), :], (rows_each, width)) for u in range(count)],
                           axis=0)


NSA_TQ = 256
NSA_TK = 512
RANK_UNROLL = 4


def _nsa_body(q_ref, ks_ref, vs_ref, kw_ref, vw_ref, kc_ref, vc_ref, gate_ref, ovt_ref, o_ref,
              vst_ref, vwt_ref, vct_ref, imp_ref, selb_ref, gt_ref, acc_ref, out_ref, *, S, gate_lane0):
    TQ, TK, H = NSA_TQ, NSA_TK, NSA_HPG
    R = H * TQ
    g = pl.program_id(1)
    qi = pl.program_id(2)
    t0 = qi * TQ
    ncp = S // NSA_CMP_STRIDE

    @pl.when(qi == 0)
    def _():
        def tr(n, c):
            for u in range(TK // LANES):
                r0 = pl.multiple_of(n * TK + u * LANES, LANES)
                cols = slice(u * LANES, (u + 1) * LANES)
                vst_ref[n, :, cols] = _t(vs_ref[0, pl.ds(r0, LANES), :]).astype(_CD)
                vwt_ref[n * (TK // LANES) + u] = _t(vw_ref[0, pl.ds(r0, LANES), :]).astype(_CD)
            return c
        lax.fori_loop(0, S // TK, tr, 0)
        for n in range(ncp // LANES):
            cols = slice(n * LANES, (n + 1) * LANES)
            vct_ref[:, cols] = _t(vc_ref[0, 0, 0, cols, :]).astype(_CD)

    q = q_ref[0]
    qt = jnp.concatenate([_t(q[:, h * HEAD_DIM:(h + 1) * HEAD_DIM]).astype(_CD) for h in range(H)], axis=1)
    pos_l = t0 + (lax.broadcasted_iota(jnp.int32, (1, R), 1) & (TQ - 1))

    gt_ref[...] = jnp.transpose(jax.nn.sigmoid(gate_ref[0]))

    def gate_row(c):
        return jnp.concatenate([gt_ref[pl.ds(gate_lane0 + (g * H + h) * 3 + c, 1), :] for h in range(H)], axis=1)

    s_c = _dot(kc_ref[0, 0, 0], qt) * SCALE
    cmp_end = lax.broadcasted_iota(jnp.int32, (ncp, 1), 0) * NSA_CMP_STRIDE + (NSA_CMP_LEN - 1)
    valid = cmp_end <= pos_l
    s_c = jnp.where(valid, s_c, NEG_INF)
    p = jnp.where(valid, jnp.exp(s_c - jnp.max(s_c, axis=0, keepdims=True)), 0.0)
    lsum = jnp.sum(p, axis=0, keepdims=True)
    p_c = p / jnp.where(lsum > 0.0, lsum, 1.0)
    out_ref[...] = gate_row(0) * _dot(vct_ref[...], p_c.astype(_CD))

    p_g = p_c[:, 0:TQ]
    for h in range(1, H):
        p_g = p_g + p_c[:, h * TQ:(h + 1) * TQ]
    p_hi, p_lo = _split(p_g)
    ovt = ovt_ref[...]
    imp = _dot(ovt, p_hi) + _dot(ovt, p_lo)
    j_io = lax.broadcasted_iota(jnp.int32, (SEL_PAD, TQ), 0)
    cur = (t0 + lax.broadcasted_iota(jnp.int32, (SEL_PAD, TQ), 1)) // NSA_SEL_BLOCK
    forced = (j_io == 0) | (j_io == cur) | (j_io == cur - 1)
    imp = jnp.where(j_io <= cur, jnp.where(forced, FORCE_SCORE, imp), NEG_INF)
    imp_ref[...] = imp

    def rank_step(mm, cnt):
        for u in range(RANK_UNROLL):
            m = mm * RANK_UNROLL + u
            row = imp_ref[pl.ds(m, 1), :]
            ahead = (row > imp) | ((row == imp) & (j_io > m))
            cnt = cnt + jnp.where(ahead, 1.0, 0.0)
        return cnt

    n_cand = (t0 + TQ - 1) // NSA_SEL_BLOCK + 1
    rank = lax.fori_loop(0, (n_cand + RANK_UNROLL - 1) // RANK_UNROLL, rank_step, jnp.zeros((SEL_PAD, TQ), _F32))
    sel_bias = jnp.where(rank < float(NSA_TOP_N), 0.0, NEG_INF)
    selb_ref[...] = jnp.concatenate([sel_bias] * H, axis=1)

    key_io = lax.broadcasted_iota(jnp.int32, (TK, 1), 0)
    init = (jnp.full((1, R), NEG_INF, _F32), jnp.zeros((1, R), _F32))
    spb = TK // NSA_SEL_BLOCK
    n_full = t0 // TK

    def sel_scores(kb):
        k0 = pl.multiple_of(kb * TK, TK)
        s2 = _dot(ks_ref[0, pl.ds(k0, TK), :], qt) * SCALE_LOG2
        return s2 + _bias_rows(selb_ref, kb * spb, spb, NSA_SEL_BLOCK, R)

    acc_ref[...] = jnp.zeros(acc_ref.shape, _F32)
    causal = jnp.where((n_full * TK + key_io) <= pos_l, 0.0, NEG_INF)
    carry = _online_step_t(sel_scores(n_full) + causal, vst_ref[n_full], *init, acc_ref)
    _, l_s = lax.fori_loop(0, n_full, lambda kb, c: _online_step_t(sel_scores(kb), vst_ref[kb], *c, acc_ref), carry)
    out_ref[...] += gate_row(1) * (acc_ref[...] / l_s)

    wk = NSA_WINDOW + TQ
    w0 = pl.multiple_of(jnp.maximum(t0 - NSA_WINDOW, 0), TQ)
    dist = pos_l - (w0 + lax.broadcasted_iota(jnp.int32, (wk, 1), 0))
    in_win = (dist >= 0) & (dist < NSA_WINDOW)
    s_w = jnp.where(in_win, _dot(kw_ref[0, pl.ds(w0, wk), :], qt) * SCALE_LOG2, NEG_INF)
    p_w = jnp.exp2(s_w - jnp.max(s_w, axis=0, keepdims=True))
    l_w = jnp.sum(p_w, axis=0, keepdims=True)
    p_w = p_w.astype(_CD)
    acc_w = _dot(vwt_ref[w0 // LANES], p_w[0:LANES])
    for u in range(1, wk // LANES):
        acc_w = acc_w + _dot(vwt_ref[w0 // LANES + u], p_w[u * LANES:(u + 1) * LANES])
    out_t = out_ref[...] + gate_row(2) * (acc_w / l_w)

    for h in range(H):
        o_ref[0, :, h * HEAD_DIM:(h + 1) * HEAD_DIM] = jnp.transpose(out_t[:, h * TQ:(h + 1) * TQ]).astype(o_ref.dtype)


def _overlap_t(S):
    ncp = S // NSA_CMP_STRIDE
    nc = (S - NSA_CMP_LEN) // NSA_CMP_STRIDE + 1
    ns = S // NSA_SEL_BLOCK
    cs = np.arange(nc)[:, None] * NSA_CMP_STRIDE
    ss = np.arange(ns)[None, :] * NSA_SEL_BLOCK
    ov = np.clip(np.minimum(cs + NSA_CMP_LEN, ss + NSA_SEL_BLOCK) - np.maximum(cs, ss), 0, None) / NSA_CMP_LEN
    out = np.zeros((SEL_PAD, ncp), np.float32)
    out[:ns, :nc] = ov.T
    return out


def _nsa_attention(zb3, kvc, zr3, gate_col, S):
    B = zb3.shape[0]
    G, TQ, TK = NSA_KV_HEADS, NSA_TQ, NSA_TK
    assert S // NSA_SEL_BLOCK <= SEL_PAD and S % TK == 0 and TK % TQ == 0 and S >= NSA_WINDOW + TQ
    ncp = S // NSA_CMP_STRIDE
    assert ncp % LANES == 0
    qw = NSA_HPG * HEAD_DIM
    R = NSA_HPG * TQ
    ovt = jnp.asarray(_overlap_t(S)).astype(_CD)
    resident = lambda col: pl.BlockSpec((1, S, HEAD_DIM), lambda b, g, i: (b, 0, col // HEAD_DIM + g))
    return pl.pallas_call(
        functools.partial(_nsa_body, S=S, gate_lane0=gate_col % LANES),
        grid=(B, G, S // TQ),
        in_specs=[pl.BlockSpec((1, TQ, qw), lambda b, g, i: (b, i, g)),
                  resident(COL_KS), resident(COL_VS), resident(COL_KW), resident(COL_VW),
                  pl.BlockSpec((1, 1, 1, ncp, HEAD_DIM), lambda b, g, i: (b, 0, g, 0, 0)),
                  pl.BlockSpec((1, 1, 1, ncp, HEAD_DIM), lambda b, g, i: (b, 1, g, 0, 0)),
                  pl.BlockSpec((1, TQ, LANES), lambda b, g, i: (b, i, gate_col // LANES)),
                  pl.BlockSpec((SEL_PAD, ncp), lambda b, g, i: (0, 0))],
        out_specs=pl.BlockSpec((1, TQ, qw), lambda b, g, i: (b, i, g)),
        out_shape=jax.ShapeDtypeStruct((B, S, NSA_Q_W), _CD),
        scratch_shapes=[pltpu.VMEM((S // TK, HEAD_DIM, TK), _CD),
                        pltpu.VMEM((S // LANES, HEAD_DIM, LANES), _CD),
                        pltpu.VMEM((HEAD_DIM, ncp), _CD),
                        pltpu.VMEM((SEL_PAD, TQ), _F32),
                        pltpu.VMEM((SEL_PAD, R), _F32),
                        pltpu.VMEM((LANES, TQ), _F32),
                        pltpu.VMEM((HEAD_DIM, R), _F32),
                        pltpu.VMEM((HEAD_DIM, R), _F32)],
        compiler_params=_params(("parallel", "parallel", "arbitrary")),
        name="nsa_attention",
    )(zb3, zb3, zb3, zb3, zb3, kvc, kvc, zr3, ovt)


MOBA_NB = 4
MOBA_TQ = 512


def _moba_body(q_ref, k_ref, v_ref, o_ref, kmean_ref, gate_ref, selb_ref, vt_ref, acc_ref, *, S):
    TQ, NB = MOBA_TQ, MOBA_NB
    TK = NB * MOBA_BLOCK
    nf = S // MOBA_BLOCK
    nfp = kmean_ref.shape[0]
    qi = pl.program_id(2)
    t0 = pl.multiple_of(qi * TQ, TQ)

    @pl.when(qi == 0)
    def _():
        kmean_ref[...] = jnp.zeros(kmean_ref.shape, _F32)
        for n in range(nf):
            rows = slice(n * MOBA_BLOCK, (n + 1) * MOBA_BLOCK)
            kmean_ref[n:n + 1, :] = jnp.mean(k_ref[0, rows, :].astype(_F32), axis=0, keepdims=True)
            cols = slice((n % NB) * MOBA_BLOCK, (n % NB + 1) * MOBA_BLOCK)
            vt_ref[n // NB, :, cols] = _t(v_ref[0, rows, :]).astype(_CD)

    qt = _t(q_ref[0]).astype(_CD)
    km_hi, km_lo = _split(kmean_ref[...])
    gate = _dot(km_hi, qt) + _dot(km_lo, qt)
    n_io = lax.broadcasted_iota(jnp.int32, (nfp, TQ), 0)
    qb = (t0 + lax.broadcasted_iota(jnp.int32, (nfp, TQ), 1)) // MOBA_BLOCK
    gate = jnp.where(n_io < qb, gate, NEG_INF)
    gate_ref[...] = gate

    def rank_step(mm, cnt):
        for u in range(RANK_UNROLL):
            m = mm * RANK_UNROLL + u
            row = gate_ref[pl.ds(m, 1), :]
            ahead = (row > gate) | ((row == gate) & (n_io > m))
            cnt = cnt + jnp.where(ahead, 1.0, 0.0)
        return cnt

    n_past = (t0 + TQ - 1) // MOBA_BLOCK
    rank = lax.fori_loop(0, (n_past + RANK_UNROLL - 1) // RANK_UNROLL, rank_step, jnp.zeros((nfp, TQ), _F32))
    keep = ((rank < float(MOBA_TOPK)) & (n_io < qb)) | (n_io == qb)
    selb_ref[...] = jnp.where(keep, 0.0, NEG_INF)

    pos_l = t0 + lax.broadcasted_iota(jnp.int32, (1, TQ), 1)
    key_io = lax.broadcasted_iota(jnp.int32, (TK, 1), 0)
    init = (jnp.full((1, TQ), NEG_INF, _F32), jnp.zeros((1, TQ), _F32))
    n_full = t0 // TK

    def scores(kb):
        k0 = pl.multiple_of(kb * TK, TK)
        s2 = _dot(k_ref[0, pl.ds(k0, TK), :], qt) * SCALE_LOG2
        return s2 + _bias_rows(selb_ref, kb * NB, NB, MOBA_BLOCK, TQ)

    acc_ref[...] = jnp.zeros(acc_ref.shape, _F32)
    causal = jnp.where((n_full * TK + key_io) <= pos_l, 0.0, NEG_INF)
    carry = _online_step_t(scores(n_full) + causal, vt_ref[n_full], *init, acc_ref)
    _, l = lax.fori_loop(0, n_full, lambda kb, c: _online_step_t(scores(kb), vt_ref[kb], *c, acc_ref), carry)
    o_ref[0] = jnp.transpose(acc_ref[...] / l).astype(o_ref.dtype)


def _moba_attention(zb3, S):
    B = zb3.shape[0]
    H, TQ = MOBA_HEADS, MOBA_TQ
    assert S % (MOBA_NB * MOBA_BLOCK) == 0 and (MOBA_NB * MOBA_BLOCK) % TQ == 0 and TQ % MOBA_BLOCK == 0
    nf = S // MOBA_BLOCK
    nfp = -(-nf // 8) * 8
    assert nfp % RANK_UNROLL == 0
    tile = lambda col: pl.BlockSpec((1, TQ, HEAD_DIM), lambda b, h, i: (b, i, col // HEAD_DIM + h))
    resident = lambda col: pl.BlockSpec((1, S, HEAD_DIM), lambda b, h, i: (b, 0, col // HEAD_DIM + h))
    return pl.pallas_call(
        functools.partial(_moba_body, S=S),
        grid=(B, H, S // TQ),
        in_specs=[tile(COL_QB), resident(COL_KB), resident(COL_VB)],
        out_specs=pl.BlockSpec((1, TQ, HEAD_DIM), lambda b, h, i: (b, i, h)),
        out_shape=jax.ShapeDtypeStruct((B, S, MOBA_W), _CD),
        scratch_shapes=[pltpu.VMEM((nfp, HEAD_DIM), _F32),
                        pltpu.VMEM((nfp, TQ), _F32),
                        pltpu.VMEM((nfp, TQ), _F32),
                        pltpu.VMEM((nf // MOBA_NB, HEAD_DIM, MOBA_NB * MOBA_BLOCK), _CD),
                        pltpu.VMEM((HEAD_DIM, TQ), _F32)],
        compiler_params=_params(("parallel", "parallel", "arbitrary")),
        name="moba_attention",
    )(zb3, zb3, zb3)


def _router_body(h_ref, g_ref, wr_ref, f_ref, eid_ref, wts_ref):
    x = h_ref[...]
    y = x * lax.rsqrt(jnp.mean(x * x, axis=-1, keepdims=True) + NORM_EPS)
    f = y * g_ref[...]
    f_ref[...] = f
    f_hi, f_lo = _split(f)
    w_hi, w_lo = _split(wr_ref[...])
    logits = _dot(f_hi, w_hi) + (_dot(f_hi, w_lo) + _dot(f_lo, w_hi))
    lane = lax.broadcasted_iota(jnp.int32, logits.shape, 1)
    big = jnp.int32(2 * LANES)
    in_g = lane < MOE_GROUPS
    lg = jnp.where(in_g, logits, -jnp.inf)
    mg = jnp.max(lg, axis=1, keepdims=True)
    pg_top = 1.0 / jnp.sum(jnp.where(in_g, jnp.exp(logits - mg), 0.0), axis=1, keepdims=True)
    grp = jnp.min(jnp.where(lg == mg, lane, big), axis=1, keepdims=True)
    eidx = lane - MOE_GROUPS
    in_e = (eidx >= 0) & (eidx < MOE_EXPERTS) & ((eidx // MOE_EXPERTS_PER_GROUP) == grp)
    le = jnp.where(in_e, logits, -jnp.inf)
    m1 = jnp.max(le, axis=1, keepdims=True)
    i1 = jnp.min(jnp.where(le == m1, lane, big), axis=1, keepdims=True)
    le2 = jnp.where(lane == i1, -jnp.inf, le)
    m2 = jnp.max(le2, axis=1, keepdims=True)
    i2 = jnp.min(jnp.where(le2 == m2, lane, big), axis=1, keepdims=True)
    e2 = jnp.exp(m2 - m1)
    w1 = pg_top / (1.0 + e2)
    w2 = pg_top * e2 / (1.0 + e2)
    eid_ref[...] = jnp.where(lane == 0, i1 - MOE_GROUPS, jnp.where(lane == 1, i2 - MOE_GROUPS, 0))
    wts_ref[...] = jnp.where(lane == 0, w1, jnp.where(lane == 1, w2, 0.0))


def _norm_router(h, g, wr, tm=256):
    T, D = h.shape
    row = lambda w: pl.BlockSpec((tm, w), lambda i: (i, 0))
    return pl.pallas_call(
        _router_body,
        grid=(T // tm,),
        in_specs=[row(D), pl.BlockSpec((1, D), lambda i: (0, 0)), pl.BlockSpec((D, LANES), lambda i: (0, 0))],
        out_specs=[row(D), row(LANES), row(LANES)],
        out_shape=[jax.ShapeDtypeStruct((T, D), _F32), jax.ShapeDtypeStruct((T, LANES), jnp.int32),
                   jax.ShapeDtypeStruct((T, LANES), _F32)],
        compiler_params=_params(("parallel",)),
        name="ffn_norm_router",
    )(h, g.reshape(1, D), wr)


MOE_ROWS = 256


def _expert_body(blk_e_ref, nused_ref, gsrc_ref, sdst_ref, f_hbm, w_ref, wg_ref, wu_ref, wd_ref, y_hbm,
                 xbuf, obuf, sem_in, sem_out):
    i = pl.program_id(0)
    nused = nused_ref[0]
    slot = i % 2
    other = 1 - slot

    def gather(blk, s, r):
        return pltpu.make_async_copy(f_hbm.at[pl.ds(gsrc_ref[blk * MOE_ROWS + r], 1)], xbuf.at[s, pl.ds(r, 1)],
                                     sem_in.at[s])

    def scatter(blk1, s, r):
        return pltpu.make_async_copy(obuf.at[s, pl.ds(r, 1)], y_hbm.at[pl.ds(sdst_ref[blk1 * MOE_ROWS + r], 1)],
                                     sem_out.at[s])

    def wait_gather(s):
        pltpu.make_async_copy(f_hbm.at[pl.ds(0, MOE_ROWS)], xbuf.at[s], sem_in.at[s]).wait()

    def wait_scatter(s):
        pltpu.make_async_copy(obuf.at[s], y_hbm.at[pl.ds(0, MOE_ROWS)], sem_out.at[s]).wait()

    @pl.when(i == 0)
    def _():
        obuf[...] = jnp.zeros(obuf.shape, _F32)
        spare0 = pltpu.make_async_copy(obuf.at[0], y_hbm.at[pl.ds(y_hbm.shape[0] - 2 * MOE_ROWS, MOE_ROWS)],
                                       sem_out.at[0])
        spare0.start()
        spare0.wait()

        def start(r, c):
            gather(0, 0, r).start()
            return c
        lax.fori_loop(0, MOE_ROWS, start, 0)

    @pl.when(i <= nused)
    def _():
        wait_gather(slot)

        @pl.when(i >= 1)
        def _():
            wait_scatter(slot)

        for r in range(MOE_ROWS):
            gather(i + 1, other, r).start()
            scatter(i, other, r).start()

        x = xbuf[slot].astype(_CD)
        hb = (jax.nn.silu(_dot(x, wg_ref[0])) * _dot(x, wu_ref[0])).astype(_CD)
        obuf[slot] = _dot(hb, wd_ref[0]) * w_ref[...]

        @pl.when(i == nused)
        def _():
            wait_scatter(other)
            wait_gather(other)


def _experts(f, blk_e, nused, gsrc, sdst, buf_w, wg, wu, wd):
    T, D = f.shape
    E, _, FF = wg.shape
    n_blocks = blk_e.shape[0]
    once = pl.Buffered(1)
    grid_spec = pltpu.PrefetchScalarGridSpec(
        num_scalar_prefetch=4,
        grid=(n_blocks,),
        in_specs=[pl.BlockSpec(memory_space=pl.ANY),
                  pl.BlockSpec((MOE_ROWS, 1), lambda i, be, nu, gs, sd: (i, 0)),
                  pl.BlockSpec((1, D, FF), lambda i, be, nu, gs, sd: (be[i], 0, 0), pipeline_mode=once),
                  pl.BlockSpec((1, D, FF), lambda i, be, nu, gs, sd: (be[i], 0, 0), pipeline_mode=once),
                  pl.BlockSpec((1, FF, D), lambda i, be, nu, gs, sd: (be[i], 0, 0), pipeline_mode=once)],
        out_specs=pl.BlockSpec(memory_space=pl.ANY),
        scratch_shapes=[pltpu.VMEM((2, MOE_ROWS, D), _F32), pltpu.VMEM((2, MOE_ROWS, D), _F32),
                        pltpu.SemaphoreType.DMA((2,)), pltpu.SemaphoreType.DMA((2,))],
    )
    return pl.pallas_call(
        _expert_body,
        grid_spec=grid_spec,
        out_shape=jax.ShapeDtypeStruct((MOE_TOPK * T + 2 * MOE_ROWS, D), _F32),
        compiler_params=_params(("arbitrary",), VMEM_LIMIT_EXPERTS),
        name="moe_experts",
    )(blk_e, nused, gsrc, sdst, f, buf_w, wg, wu, wd)


def _group_by_expert(eid, wts):
    T = eid.shape[0]
    n_assign = T * MOE_TOPK
    n_blocks = -(-n_assign // MOE_ROWS) + MOE_EXPERTS
    e_flat = eid.reshape(-1)
    order = jnp.argsort(e_flat, stable=True).astype(jnp.int32)
    counts = jnp.bincount(e_flat, length=MOE_EXPERTS).astype(jnp.int32)
    starts = jnp.cumsum(counts) - counts
    pcounts = (counts + MOE_ROWS - 1) // MOE_ROWS * MOE_ROWS
    pends = jnp.cumsum(pcounts)
    pstarts = pends - pcounts
    blk_start = jnp.arange(n_blocks, dtype=jnp.int32) * MOE_ROWS
    blk_e = jnp.minimum(jnp.sum(pends[None, :] <= blk_start[:, None], axis=1), MOE_EXPERTS - 1).astype(jnp.int32)
    slot = jnp.arange(n_blocks * MOE_ROWS, dtype=jnp.int32)
    e_p = jnp.repeat(blk_e, MOE_ROWS)
    off = slot - pstarts[e_p]
    valid = (off >= 0) & (off < counts[e_p])
    src = order[jnp.clip(starts[e_p] + off, 0, n_assign - 1)]
    buf_w = jnp.where(valid, wts.reshape(-1)[src], 0.0)
    spare = n_assign + ((slot // MOE_ROWS) % 2) * MOE_ROWS + slot % MOE_ROWS
    gsrc = jnp.where(valid, src // MOE_TOPK, 0)
    sdst = jnp.where(valid, (src % MOE_TOPK) * T + src // MOE_TOPK, spare)
    gsrc = jnp.concatenate([gsrc, jnp.zeros((MOE_ROWS,), jnp.int32)]).astype(jnp.int32)
    sdst = jnp.concatenate([n_assign + MOE_ROWS + jnp.arange(MOE_ROWS, dtype=jnp.int32), sdst]).astype(jnp.int32)
    nused = (pends[-1:] // MOE_ROWS).astype(jnp.int32)
    return blk_e, nused, gsrc, sdst, buf_w.reshape(-1, 1)


def _addnorm_body(h_ref, y0_ref, y1_ref, g_ref, h2_ref, n_ref):
    h2 = h_ref[...] + (y0_ref[...] + y1_ref[...])
    h2_ref[...] = h2
    y = h2 * lax.rsqrt(jnp.mean(h2 * h2, axis=-1, keepdims=True) + NORM_EPS)
    n_ref[...] = (y * g_ref[...]).astype(n_ref.dtype)


def _add_norm(h, y2, g, tm=256):
    T, D = h.shape
    nt = T // tm
    return pl.pallas_call(
        _addnorm_body,
        grid=(nt,),
        in_specs=[pl.BlockSpec((tm, D), lambda i: (i, 0)),
                  pl.BlockSpec((tm, D), lambda i: (i, 0)),
                  pl.BlockSpec((tm, D), lambda i: (nt + i, 0)),
                  pl.BlockSpec((1, D), lambda i: (0, 0))],
        out_specs=[pl.BlockSpec((tm, D), lambda i: (i, 0)), pl.BlockSpec((tm, D), lambda i: (i, 0))],
        out_shape=[jax.ShapeDtypeStruct((T, D), _F32), jax.ShapeDtypeStruct((T, D), _CD)],
        compiler_params=_params(("parallel",)),
        name="moe_add_norm",
    )(h, y2, y2, g.reshape(1, D))


def kernel(x, p, attn_norm, w_in, nsa_cmp_pe, nsa_cmp_w1, nsa_cmp_w2, w_up_nsa, w_up_moba, w_out, ffn_norm, router_group, router_expert, expert_w_gate, expert_w_up, expert_w_down, ple_norm, ple_gate, ple_proj, final_norm):
    B, S, D = x.shape
    depth = w_in.shape[0]
    T = B * S
    tm = min(1024, T)
    tm_in = min(1024, S)
    tn = min(512, D)
    assert D % tn == 0 and T % tm == 0 and S % tm_in == 0

    o_kv = NSA_Q_W
    o_gate = o_kv + 6 * NSA_KV_W
    o_qkvb = o_gate + NSA_GATE_W
    o_merge = o_qkvb + 3 * MOBA_W
    col_gate = COL_MERGE + 2 * D
    n_rest = col_gate + GATE_PAD

    cos, sin = _rope_tables(jnp.arange(S))
    ccos, csin = _rope_tables(jnp.arange(S // NSA_CMP_STRIDE) * NSA_CMP_STRIDE + NSA_CMP_LEN - 1)

    h = x.reshape(T, D)
    for i in range(depth):
        wi = w_in[i]
        w_heads = jnp.concatenate([wi[:, 0:o_kv], wi[:, o_kv + 2 * NSA_KV_W:o_gate], wi[:, o_qkvb:o_merge]],
                                  axis=1).astype(_CD)
        w_rest = jnp.concatenate([wi[:, o_kv:o_kv + 2 * NSA_KV_W], wi[:, o_merge:o_merge + 2 * D],
                                  wi[:, o_gate:o_qkvb], jnp.zeros((D, GATE_PAD - NSA_GATE_W), wi.dtype)],
                                 axis=1).astype(_CD)
        a = _rmsnorm(h, attn_norm[i], _CD)
        zb3 = _head_proj(a, w_heads, cos, sin, S, tm_in).reshape(B, S, HEADS_W)
        zr = _mm_call(_mm_plain_body, [a], [w_rest], [], n_rest, _F32, tm_in, tn, "in_proj_rest")
        zr3 = zr.reshape(B, S, n_rest)
        kvc = _nsa_compress(zr3, nsa_cmp_pe[i], nsa_cmp_w1[i], nsa_cmp_w2[i], ccos, csin)
        o_a = _nsa_attention(zb3, kvc, zr3, col_gate, S).reshape(T, NSA_Q_W)
        o_b = _moba_attention(zb3, S).reshape(T, MOBA_W)
        mixed = _mm_call(_mm_gated_body, [o_a, o_b], [w_up_nsa[i].astype(_CD), w_up_moba[i].astype(_CD)],
                         [(zr, COL_MERGE), (zr, COL_MERGE + D)], D, _CD, tm, tn, "merge_up")
        h = _mm_call(_mm_resid_body, [mixed], [w_out[i].astype(_CD)], [(h, 0)], D, _F32, tm, tn, "out_proj")

        wr = jnp.concatenate([router_group[i], router_expert[i],
                              jnp.zeros((D, LANES - MOE_GROUPS - MOE_EXPERTS), _F32)], axis=1)
        f, eid, wts = _norm_router(h, ffn_norm[i], wr)
        blk_e, nused, gsrc, sdst, buf_w = _group_by_expert(eid[:, :MOE_TOPK], wts[:, :MOE_TOPK])
        y2 = _experts(f, blk_e, nused, gsrc, sdst, buf_w, expert_w_gate[i].astype(_CD),
                      expert_w_up[i].astype(_CD), expert_w_down[i].astype(_CD))
        h2, n = _add_norm(h, y2, ple_norm[i])
        h = _mm_call(_mm_ple_body, [n, p[i].reshape(T, -1)], [ple_gate[i].astype(_CD), ple_proj[i].astype(_CD)],
                     [(h2, 0)], D, _F32, tm, tn, "ple")
    return _rmsnorm(h, final_norm, _F32).reshape(B, S, D)
```

```python
import functools

import numpy as np
import jax
import jax.numpy as jnp
from jax import lax
from jax.experimental import pallas as pl
from jax.experimental.pallas import tpu as pltpu

HEAD_DIM = 128
ROPE_THETA = 10000.0
NORM_EPS = 1e-6
NEG_INF = -1e30
FORCE_SCORE = 1e9
SCALE = HEAD_DIM ** -0.5

NSA_Q_HEADS = 16
NSA_KV_HEADS = 4
NSA_HPG = NSA_Q_HEADS // NSA_KV_HEADS
NSA_CMP_LEN = 32
NSA_CMP_STRIDE = 16
NSA_SEL_BLOCK = 64
NSA_TOP_N = 16
NSA_WINDOW = 512
NSA_Q_W = NSA_Q_HEADS * HEAD_DIM
NSA_KV_W = NSA_KV_HEADS * HEAD_DIM
NSA_GATE_W = NSA_Q_HEADS * 3

MOBA_HEADS = 16
MOBA_BLOCK = 256
MOBA_TOPK = 3
MOBA_W = MOBA_HEADS * HEAD_DIM

MOE_GROUPS = 4
MOE_EXPERTS_PER_GROUP = 8
MOE_EXPERTS = MOE_GROUPS * MOE_EXPERTS_PER_GROUP
MOE_TOPK = 2

LANES = 128
SEL_PAD = 128
GATE_PAD = 512
VMEM_LIMIT = 48 * 1024 * 1024
VMEM_LIMIT_EXPERTS = 56 * 1024 * 1024

COL_QA = 0
COL_KS = COL_QA + NSA_Q_W
COL_VS = COL_KS + NSA_KV_W
COL_KW = COL_VS + NSA_KV_W
COL_VW = COL_KW + NSA_KV_W
COL_QB = COL_VW + NSA_KV_W
COL_KB = COL_QB + MOBA_W
COL_VB = COL_KB + MOBA_W
HEADS_W = COL_VB + MOBA_W
COL_KC = 0
COL_VC = COL_KC + NSA_KV_W
COL_MERGE = COL_VC + NSA_KV_W

_CD = jnp.bfloat16
_F32 = jnp.float32


def _dot(a, b):
    return jnp.dot(a, b, preferred_element_type=_F32)


def _split(x):
    hi = x.astype(_CD)
    lo = (x - hi.astype(_F32)).astype(_CD)
    return hi, lo


def _t(x):
    return jnp.transpose(x.astype(_F32))


def _params(sem, vmem=VMEM_LIMIT):
    return pltpu.CompilerParams(dimension_semantics=sem, vmem_limit_bytes=vmem)


def _rmsnorm_body(x_ref, g_ref, o_ref):
    x = x_ref[...]
    y = x * lax.rsqrt(jnp.mean(x * x, axis=-1, keepdims=True) + NORM_EPS)
    o_ref[...] = (y * g_ref[...]).astype(o_ref.dtype)


def _rmsnorm(x, g, out_dtype, tm=256):
    T, D = x.shape
    return pl.pallas_call(
        _rmsnorm_body,
        grid=(T // tm,),
        in_specs=[pl.BlockSpec((tm, D), lambda i: (i, 0)),
                  pl.BlockSpec((1, D), lambda i: (0, 0))],
        out_specs=pl.BlockSpec((tm, D), lambda i: (i, 0)),
        out_shape=jax.ShapeDtypeStruct((T, D), out_dtype),
        compiler_params=_params(("parallel",)),
        name="rmsnorm",
    )(x, g.reshape(1, D))


def _mm_call(body, lhs, rhs, extras, n_out, out_dtype, tm, tn, name):
    M = lhs[0].shape[0]
    in_specs = [pl.BlockSpec((tm, a.shape[1]), lambda i, j: (i, 0)) for a in lhs]
    in_specs += [pl.BlockSpec((w.shape[0], tn), lambda i, j: (0, j)) for w in rhs]
    args = list(lhs) + list(rhs)
    for arr, off in extras:
        assert off % tn == 0
        in_specs.append(pl.BlockSpec((tm, tn), functools.partial(lambda i, j, o: (i, j + o), o=off // tn)))
        args.append(arr)
    return pl.pallas_call(
        body,
        grid=(M // tm, n_out // tn),
        in_specs=in_specs,
        out_specs=pl.BlockSpec((tm, tn), lambda i, j: (i, j)),
        out_shape=jax.ShapeDtypeStruct((M, n_out), out_dtype),
        compiler_params=_params(("parallel", "parallel")),
        name=name,
    )(*args)


def _mm_plain_body(a_ref, w_ref, o_ref):
    o_ref[...] = _dot(a_ref[...], w_ref[...]).astype(o_ref.dtype)


def _mm_gated_body(a1_ref, a2_ref, w1_ref, w2_ref, m1_ref, m2_ref, o_ref):
    y1 = _dot(a1_ref[...], w1_ref[...])
    y2 = _dot(a2_ref[...], w2_ref[...])
    o_ref[...] = (jax.nn.sigmoid(m1_ref[...]) * y1 + jax.nn.sigmoid(m2_ref[...]) * y2).astype(o_ref.dtype)


def _mm_resid_body(a_ref, w_ref, r_ref, o_ref):
    o_ref[...] = r_ref[...] + _dot(a_ref[...], w_ref[...])


def _mm_ple_body(n_ref, p_ref, wg_ref, wp_ref, r_ref, o_ref):
    gate = jax.nn.sigmoid(_dot(n_ref[...], wg_ref[...]))
    emb = _dot(p_ref[...].astype(_CD), wp_ref[...])
    o_ref[...] = r_ref[...] + gate * emb


def _rope_tables(pos):
    half = HEAD_DIM // 2
    inv = jnp.power(ROPE_THETA, -jnp.arange(half, dtype=_F32) / half)
    ang = pos.astype(_F32)[:, None] * inv
    c, s = jnp.cos(ang), jnp.sin(ang)
    return jnp.concatenate([c, c], axis=-1), jnp.concatenate([-s, s], axis=-1)


def _rope(x, c, s):
    return x * c + pltpu.roll(x, HEAD_DIM // 2, 1) * s


HEADS_TN = 512
_ROPE_BLOCKS = ((COL_QA // HEADS_TN, COL_KS // HEADS_TN), (COL_KS // HEADS_TN, COL_VS // HEADS_TN),
                (COL_KW // HEADS_TN, COL_VW // HEADS_TN), (COL_QB // HEADS_TN, COL_VB // HEADS_TN))


def _mm_heads_body(a_ref, w_ref, cos_ref, sin_ref, o_ref):
    j = pl.program_id(1)
    y = _dot(a_ref[...], w_ref[...])
    is_rope = functools.reduce(jnp.logical_or, [(j >= a) & (j < b) for a, b in _ROPE_BLOCKS])

    @pl.when(is_rope)
    def _():
        c = cos_ref[...]
        s = sin_ref[...]
        for h in range(HEADS_TN // HEAD_DIM):
            sl = slice(h * HEAD_DIM, (h + 1) * HEAD_DIM)
            o_ref[:, sl] = _rope(y[:, sl], c, s).astype(o_ref.dtype)

    @pl.when(jnp.logical_not(is_rope))
    def _():
        o_ref[...] = y.astype(o_ref.dtype)


def _head_proj(a, w, cos, sin, S, tm):
    T, D = a.shape
    nps = S // tm
    return pl.pallas_call(
        _mm_heads_body,
        grid=(T // tm, HEADS_W // HEADS_TN),
        in_specs=[pl.BlockSpec((tm, D), lambda i, j: (i, 0)),
                  pl.BlockSpec((D, HEADS_TN), lambda i, j: (0, j)),
                  pl.BlockSpec((tm, HEAD_DIM), lambda i, j: (i % nps, 0)),
                  pl.BlockSpec((tm, HEAD_DIM), lambda i, j: (i % nps, 0))],
        out_specs=pl.BlockSpec((tm, HEADS_TN), lambda i, j: (i, j)),
        out_shape=jax.ShapeDtypeStruct((T, HEADS_W), _CD),
        compiler_params=_params(("parallel", "parallel")),
        name="in_proj_heads",
    )(a, w, cos, sin)


def _compress_body(x_ref, pe_ref, w1_ref, w2_ref, cos_ref, sin_ref, o_ref, xpad_ref, *, S, nc):
    ncp = S // NSA_CMP_STRIDE
    xpad_ref[0:S, :] = x_ref[0]
    xpad_ref[S:S + NSA_CMP_LEN, :] = jnp.zeros((NSA_CMP_LEN, HEAD_DIM), _F32)
    acc = jnp.zeros((ncp, HEAD_DIM), _F32)
    for l in range(NSA_CMP_LEN):
        xl = xpad_ref[pl.ds(l, ncp, stride=NSA_CMP_STRIDE), :] + pe_ref[0, l:l + 1, :]
        acc = acc + _dot(xl.astype(_CD), w1_ref[0, l].astype(_CD))
    hid = jax.nn.gelu(acc)
    out = _dot(hid.astype(_CD), w2_ref[0].astype(_CD))
    is_key = pl.program_id(1) == 0
    out = jnp.where(is_key, _rope(out, cos_ref[...], sin_ref[...]), out)
    row = lax.broadcasted_iota(jnp.int32, (ncp, HEAD_DIM), 0)
    o_ref[0, 0, 0] = jnp.where(row < nc, out, 0.0).astype(o_ref.dtype)


def _nsa_compress(zr3, pe, w1, w2, ccos, csin):
    B, S, _ = zr3.shape
    G = NSA_KV_HEADS
    ncp = S // NSA_CMP_STRIDE
    nc = (S - NSA_CMP_LEN) // NSA_CMP_STRIDE + 1
    cb = COL_KC // HEAD_DIM
    return pl.pallas_call(
        functools.partial(_compress_body, S=S, nc=nc),
        grid=(B, 2, G),
        in_specs=[pl.BlockSpec((1, S, HEAD_DIM), lambda b, t, g: (b, 0, cb + t * G + g)),
                  pl.BlockSpec((1, NSA_CMP_LEN, HEAD_DIM), lambda b, t, g: (t, 0, 0)),
                  pl.BlockSpec((1, NSA_CMP_LEN, HEAD_DIM, HEAD_DIM), lambda b, t, g: (t, 0, 0, 0)),
                  pl.BlockSpec((1, HEAD_DIM, HEAD_DIM), lambda b, t, g: (t, 0, 0)),
                  pl.BlockSpec((ncp, HEAD_DIM), lambda b, t, g: (0, 0)),
                  pl.BlockSpec((ncp, HEAD_DIM), lambda b, t, g: (0, 0))],
        out_specs=pl.BlockSpec((1, 1, 1, ncp, HEAD_DIM), lambda b, t, g: (b, t, g, 0, 0)),
        out_shape=jax.ShapeDtypeStruct((B, 2, G, ncp, HEAD_DIM), _CD),
        scratch_shapes=[pltpu.VMEM((S + NSA_CMP_LEN, HEAD_DIM), _F32)],
        compiler_params=_params(("parallel", "parallel", "parallel")),
        name="nsa_compress",
    )(zr3, pe, w1, w2, ccos, csin)


SCALE_LOG2 = SCALE * 1.4426950408889634


def _online_step_t(s2, vt, m, l, acc_ref, allowed=None):
    m_new = jnp.maximum(m, jnp.max(s2, axis=0, keepdims=True))
    alpha = jnp.exp2(m - m_new)
    p = jnp.exp2(s2 - m_new)
    if allowed is not None:
        p = jnp.where(allowed, p, 0.0)
    l = alpha * l + jnp.sum(p, axis=0, keepdims=True)
    acc_ref[...] = alpha * acc_ref[...] + _dot(vt, p.astype(_CD))
    return m_new, l


def _next_below(x):
    bits = pltpu.bitcast(x, jnp.int32)
    stepped = pltpu.bitcast(jnp.where(bits > 0, bits - 1, bits + 1), _F32)
    return jnp.where(x == 0.0, -1e-30, stepped)


def _rank_rows(val_ref, thr_ref, n_rows):
    thr_ref[...] = _next_below(val_ref[...])

    def step(mm, cnt):
        for u in range(RANK_UNROLL):
            m = mm * RANK_UNROLL + u
            row = val_ref[pl.ds(m, 1), :]
            thr_ref[pl.ds(m, 1), :] = row
            cnt = cnt + jnp.where(row > thr_ref[...], 1.0, 0.0)
        return cnt

    return lax.fori_loop(0, (n_rows + RANK_UNROLL - 1) // RANK_UNROLL, step, jnp.zeros(val_ref.shape, _F32))


def _bias_rows(ref, first, count, rows_each, width):
    return jnp.concatenate([jnp.broadcast_to(ref[pl.ds(first + u, 1), :], (rows_each, width)) for u in range(count)],
                           axis=0)


NSA_TQ = 256
NSA_TK = 512
RANK_UNROLL = 4


def _nsa_body(q_ref, ks_ref, vs_ref, kw_ref, vw_ref, kc_ref, vc_ref, gate_ref, ovt_ref, o_ref,
              vst_ref, vwt_ref, vct_ref, imp_ref, thr_ref, selb_ref, gt_ref, acc_ref, out_ref, *, S, gate_lane0):
    TQ, TK, H = NSA_TQ, NSA_TK, NSA_HPG
    R = H * TQ
    g = pl.program_id(1)
    qi = pl.program_id(2)
    t0 = qi * TQ
    ncp = S // NSA_CMP_STRIDE

    @pl.when(qi == 0)
    def _():
        def tr(n, c):
            for u in range(TK // LANES):
                r0 = pl.multiple_of(n * TK + u * LANES, LANES)
                cols = slice(u * LANES, (u + 1) * LANES)
                vst_ref[n, :, cols] = _t(vs_ref[0, pl.ds(r0, LANES), :]).astype(_CD)
                vwt_ref[n * (TK // LANES) + u] = _t(vw_ref[0, pl.ds(r0, LANES), :]).astype(_CD)
            return c
        lax.fori_loop(0, S // TK, tr, 0)
        for n in range(ncp // LANES):
            cols = slice(n * LANES, (n + 1) * LANES)
            vct_ref[:, cols] = _t(vc_ref[0, 0, 0, cols, :]).astype(_CD)

    q = q_ref[0]
    qt = jnp.concatenate([_t(q[:, h * HEAD_DIM:(h + 1) * HEAD_DIM]).astype(_CD) for h in range(H)], axis=1)
    pos_l = t0 + (lax.broadcasted_iota(jnp.int32, (1, R), 1) & (TQ - 1))

    gt_ref[...] = jnp.transpose(jax.nn.sigmoid(gate_ref[0]))

    def gate_row(c):
        return jnp.concatenate([gt_ref[pl.ds(gate_lane0 + (g * H + h) * 3 + c, 1), :] for h in range(H)], axis=1)

    s_c = _dot(kc_ref[0, 0, 0], qt) * SCALE
    cmp_end = lax.broadcasted_iota(jnp.int32, (ncp, 1), 0) * NSA_CMP_STRIDE + (NSA_CMP_LEN - 1)
    valid = cmp_end <= pos_l
    s_c = jnp.where(valid, s_c, NEG_INF)
    p = jnp.where(valid, jnp.exp(s_c - jnp.max(s_c, axis=0, keepdims=True)), 0.0)
    lsum = jnp.sum(p, axis=0, keepdims=True)
    p_c = p / jnp.where(lsum > 0.0, lsum, 1.0)
    out_ref[...] = gate_row(0) * _dot(vct_ref[...], p_c.astype(_CD))

    p_g = p_c[:, 0:TQ]
    for h in range(1, H):
        p_g = p_g + p_c[:, h * TQ:(h + 1) * TQ]
    p_hi, p_lo = _split(p_g)
    ovt = ovt_ref[...]
    imp = _dot(ovt, p_hi) + _dot(ovt, p_lo)
    j_io = lax.broadcasted_iota(jnp.int32, (SEL_PAD, TQ), 0)
    cur = (t0 + lax.broadcasted_iota(jnp.int32, (SEL_PAD, TQ), 1)) // NSA_SEL_BLOCK
    forced = (j_io == 0) | (j_io == cur) | (j_io == cur - 1)
    imp = jnp.where(j_io <= cur, jnp.where(forced, FORCE_SCORE, imp), NEG_INF)
    imp_ref[...] = imp

    rank = _rank_rows(imp_ref, thr_ref, (t0 + TQ - 1) // NSA_SEL_BLOCK + 1)
    sel_bias = jnp.where(rank < float(NSA_TOP_N), 0.0, NEG_INF)
    selb_ref[...] = jnp.concatenate([sel_bias] * H, axis=1)

    key_io = lax.broadcasted_iota(jnp.int32, (TK, 1), 0)
    init = (jnp.full((1, R), NEG_INF, _F32), jnp.zeros((1, R), _F32))
    spb = TK // NSA_SEL_BLOCK
    n_full = t0 // TK

    def sel_scores(kb):
        k0 = pl.multiple_of(kb * TK, TK)
        s2 = _dot(ks_ref[0, pl.ds(k0, TK), :], qt) * SCALE_LOG2
        return s2 + _bias_rows(selb_ref, kb * spb, spb, NSA_SEL_BLOCK, R)

    acc_ref[...] = jnp.zeros(acc_ref.shape, _F32)
    causal = jnp.where((n_full * TK + key_io) <= pos_l, 0.0, NEG_INF)
    carry = _online_step_t(sel_scores(n_full) + causal, vst_ref[n_full], *init, acc_ref)
    _, l_s = lax.fori_loop(0, n_full, lambda kb, c: _online_step_t(sel_scores(kb), vst_ref[kb], *c, acc_ref), carry)
    out_ref[...] += gate_row(1) * (acc_ref[...] / l_s)

    wk = NSA_WINDOW + TQ
    w0 = pl.multiple_of(jnp.maximum(t0 - NSA_WINDOW, 0), TQ)
    dist = pos_l - (w0 + lax.broadcasted_iota(jnp.int32, (wk, 1), 0))
    in_win = (dist >= 0) & (dist < NSA_WINDOW)
    s_w = jnp.where(in_win, _dot(kw_ref[0, pl.ds(w0, wk), :], qt) * SCALE_LOG2, NEG_INF)
    p_w = jnp.exp2(s_w - jnp.max(s_w, axis=0, keepdims=True))
    l_w = jnp.sum(p_w, axis=0, keepdims=True)
    p_w = p_w.astype(_CD)
    acc_w = _dot(vwt_ref[w0 // LANES], p_w[0:LANES])
    for u in range(1, wk // LANES):
        acc_w = acc_w + _dot(vwt_ref[w0 // LANES + u], p_w[u * LANES:(u + 1) * LANES])
    out_t = out_ref[...] + gate_row(2) * (acc_w / l_w)

    for h in range(H):
        o_ref[0, :, h * HEAD_DIM:(h + 1) * HEAD_DIM] = jnp.transpose(out_t[:, h * TQ:(h + 1) * TQ]).astype(o_ref.dtype)


def _overlap_t(S):
    ncp = S // NSA_CMP_STRIDE
    nc = (S - NSA_CMP_LEN) // NSA_CMP_STRIDE + 1
    ns = S // NSA_SEL_BLOCK
    cs = np.arange(nc)[:, None] * NSA_CMP_STRIDE
    ss = np.arange(ns)[None, :] * NSA_SEL_BLOCK
    ov = np.clip(np.minimum(cs + NSA_CMP_LEN, ss + NSA_SEL_BLOCK) - np.maximum(cs, ss), 0, None) / NSA_CMP_LEN
    out = np.zeros((SEL_PAD, ncp), np.float32)
    out[:ns, :nc] = ov.T
    return out


def _nsa_attention(zb3, kvc, zr3, gate_col, S):
    B = zb3.shape[0]
    G, TQ, TK = NSA_KV_HEADS, NSA_TQ, NSA_TK
    assert S // NSA_SEL_BLOCK <= SEL_PAD and S % TK == 0 and TK % TQ == 0 and S >= NSA_WINDOW + TQ
    ncp = S // NSA_CMP_STRIDE
    assert ncp % LANES == 0
    qw = NSA_HPG * HEAD_DIM
    R = NSA_HPG * TQ
    ovt = jnp.asarray(_overlap_t(S)).astype(_CD)
    resident = lambda col: pl.BlockSpec((1, S, HEAD_DIM), lambda b, g, i: (b, 0, col // HEAD_DIM + g))
    return pl.pallas_call(
        functools.partial(_nsa_body, S=S, gate_lane0=gate_col % LANES),
        grid=(B, G, S // TQ),
        in_specs=[pl.BlockSpec((1, TQ, qw), lambda b, g, i: (b, i, g)),
                  resident(COL_KS), resident(COL_VS), resident(COL_KW), resident(COL_VW),
                  pl.BlockSpec((1, 1, 1, ncp, HEAD_DIM), lambda b, g, i: (b, 0, g, 0, 0)),
                  pl.BlockSpec((1, 1, 1, ncp, HEAD_DIM), lambda b, g, i: (b, 1, g, 0, 0)),
                  pl.BlockSpec((1, TQ, LANES), lambda b, g, i: (b, i, gate_col // LANES)),
                  pl.BlockSpec((SEL_PAD, ncp), lambda b, g, i: (0, 0))],
        out_specs=pl.BlockSpec((1, TQ, qw), lambda b, g, i: (b, i, g)),
        out_shape=jax.ShapeDtypeStruct((B, S, NSA_Q_W), _CD),
        scratch_shapes=[pltpu.VMEM((S // TK, HEAD_DIM, TK), _CD),
                        pltpu.VMEM((S // LANES, HEAD_DIM, LANES), _CD),
                        pltpu.VMEM((HEAD_DIM, ncp), _CD),
                        pltpu.VMEM((SEL_PAD, TQ), _F32),
                        pltpu.VMEM((SEL_PAD, TQ), _F32),
                        pltpu.VMEM((SEL_PAD, R), _F32),
                        pltpu.VMEM((LANES, TQ), _F32),
                        pltpu.VMEM((HEAD_DIM, R), _F32),
                        pltpu.VMEM((HEAD_DIM, R), _F32)],
        compiler_params=_params(("parallel", "parallel", "arbitrary")),
        name="nsa_attention",
    )(zb3, zb3, zb3, zb3, zb3, kvc, kvc, zr3, ovt)


MOBA_NB = 4
MOBA_TQ = 1024


def _moba_body(q_ref, k_ref, v_ref, o_ref, kmean_ref, gate_ref, thr_ref, selb_ref, vt_ref, acc_ref, *, S):
    TQ, NB = MOBA_TQ, MOBA_NB
    TK = NB * MOBA_BLOCK
    nf = S // MOBA_BLOCK
    nfp = kmean_ref.shape[0]
    qi = pl.program_id(2)
    t0 = pl.multiple_of(qi * TQ, TQ)

    @pl.when(qi == 0)
    def _():
        kmean_ref[...] = jnp.zeros(kmean_ref.shape, _F32)
        for n in range(nf):
            rows = slice(n * MOBA_BLOCK, (n + 1) * MOBA_BLOCK)
            kmean_ref[n:n + 1, :] = jnp.mean(k_ref[0, rows, :].astype(_F32), axis=0, keepdims=True)
            cols = slice((n % NB) * MOBA_BLOCK, (n % NB + 1) * MOBA_BLOCK)
            vt_ref[n // NB, :, cols] = _t(v_ref[0, rows, :]).astype(_CD)

    qt = _t(q_ref[0]).astype(_CD)
    km_hi, km_lo = _split(kmean_ref[...])
    gate = _dot(km_hi, qt) + _dot(km_lo, qt)
    n_io = lax.broadcasted_iota(jnp.int32, (nfp, TQ), 0)
    qb = (t0 + lax.broadcasted_iota(jnp.int32, (nfp, TQ), 1)) // MOBA_BLOCK
    gate = jnp.where(n_io < qb, gate, NEG_INF)
    gate_ref[...] = gate

    rank = _rank_rows(gate_ref, thr_ref, (t0 + TQ - 1) // MOBA_BLOCK)
    keep = ((rank < float(MOBA_TOPK)) & (n_io < qb)) | (n_io == qb)
    selb_ref[...] = jnp.where(keep, 0.0, NEG_INF)

    pos_l = t0 + lax.broadcasted_iota(jnp.int32, (1, TQ), 1)
    key_io = lax.broadcasted_iota(jnp.int32, (TK, 1), 0)
    init = (jnp.full((1, TQ), NEG_INF, _F32), jnp.zeros((1, TQ), _F32))
    n_full = t0 // TK

    def scores(kb):
        k0 = pl.multiple_of(kb * TK, TK)
        s2 = _dot(k_ref[0, pl.ds(k0, TK), :], qt) * SCALE_LOG2
        return s2 + _bias_rows(selb_ref, kb * NB, NB, MOBA_BLOCK, TQ)

    acc_ref[...] = jnp.zeros(acc_ref.shape, _F32)
    causal = jnp.where((n_full * TK + key_io) <= pos_l, 0.0, NEG_INF)
    carry = _online_step_t(scores(n_full) + causal, vt_ref[n_full], *init, acc_ref)
    _, l = lax.fori_loop(0, n_full, lambda kb, c: _online_step_t(scores(kb), vt_ref[kb], *c, acc_ref), carry)
    o_ref[0] = jnp.transpose(acc_ref[...] / l).astype(o_ref.dtype)


def _moba_attention(zb3, S):
    B = zb3.shape[0]
    H, TQ = MOBA_HEADS, MOBA_TQ
    assert S % (MOBA_NB * MOBA_BLOCK) == 0 and (MOBA_NB * MOBA_BLOCK) % TQ == 0 and TQ % MOBA_BLOCK == 0
    nf = S // MOBA_BLOCK
    nfp = -(-nf // 8) * 8
    assert nfp % RANK_UNROLL == 0
    tile = lambda col: pl.BlockSpec((1, TQ, HEAD_DIM), lambda b, h, i: (b, i, col // HEAD_DIM + h))
    resident = lambda col: pl.BlockSpec((1, S, HEAD_DIM), lambda b, h, i: (b, 0, col // HEAD_DIM + h))
    return pl.pallas_call(
        functools.partial(_moba_body, S=S),
        grid=(B, H, S // TQ),
        in_specs=[tile(COL_QB), resident(COL_KB), resident(COL_VB)],
        out_specs=pl.BlockSpec((1, TQ, HEAD_DIM), lambda b, h, i: (b, i, h)),
        out_shape=jax.ShapeDtypeStruct((B, S, MOBA_W), _CD),
        scratch_shapes=[pltpu.VMEM((nfp, HEAD_DIM), _F32),
                        pltpu.VMEM((nfp, TQ), _F32),
                        pltpu.VMEM((nfp, TQ), _F32),
                        pltpu.VMEM((nfp, TQ), _F32),
                        pltpu.VMEM((nf // MOBA_NB, HEAD_DIM, MOBA_NB * MOBA_BLOCK), _CD),
                        pltpu.VMEM((HEAD_DIM, TQ), _F32)],
        compiler_params=_params(("parallel", "parallel", "arbitrary")),
        name="moba_attention",
    )(zb3, zb3, zb3)


def _router_body(h_ref, g_ref, wr_ref, f_ref, eid_ref, wts_ref):
    x = h_ref[...]
    y = x * lax.rsqrt(jnp.mean(x * x, axis=-1, keepdims=True) + NORM_EPS)
    f = y * g_ref[...]
    f_ref[...] = f
    f_hi, f_lo = _split(f)
    w_hi, w_lo = _split(wr_ref[...])
    logits = _dot(f_hi, w_hi) + (_dot(f_hi, w_lo) + _dot(f_lo, w_hi))
    lane = lax.broadcasted_iota(jnp.int32, logits.shape, 1)
    big = jnp.int32(2 * LANES)
    in_g = lane < MOE_GROUPS
    lg = jnp.where(in_g, logits, -jnp.inf)
    mg = jnp.max(lg, axis=1, keepdims=True)
    pg_top = 1.0 / jnp.sum(jnp.where(in_g, jnp.exp(logits - mg), 0.0), axis=1, keepdims=True)
    grp = jnp.min(jnp.where(lg == mg, lane, big), axis=1, keepdims=True)
    eidx = lane - MOE_GROUPS
    in_e = (eidx >= 0) & (eidx < MOE_EXPERTS) & ((eidx // MOE_EXPERTS_PER_GROUP) == grp)
    le = jnp.where(in_e, logits, -jnp.inf)
    m1 = jnp.max(le, axis=1, keepdims=True)
    i1 = jnp.min(jnp.where(le == m1, lane, big), axis=1, keepdims=True)
    le2 = jnp.where(lane == i1, -jnp.inf, le)
    m2 = jnp.max(le2, axis=1, keepdims=True)
    i2 = jnp.min(jnp.where(le2 == m2, lane, big), axis=1, keepdims=True)
    e2 = jnp.exp(m2 - m1)
    w1 = pg_top / (1.0 + e2)
    w2 = pg_top * e2 / (1.0 + e2)
    eid_ref[...] = jnp.where(lane == 0, i1 - MOE_GROUPS, jnp.where(lane == 1, i2 - MOE_GROUPS, 0))
    wts_ref[...] = jnp.where(lane == 0, w1, jnp.where(lane == 1, w2, 0.0))


def _norm_router(h, g, wr, tm=256):
    T, D = h.shape
    row = lambda w: pl.BlockSpec((tm, w), lambda i: (i, 0))
    return pl.pallas_call(
        _router_body,
        grid=(T // tm,),
        in_specs=[row(D), pl.BlockSpec((1, D), lambda i: (0, 0)), pl.BlockSpec((D, LANES), lambda i: (0, 0))],
        out_specs=[row(D), row(LANES), row(LANES)],
        out_shape=[jax.ShapeDtypeStruct((T, D), _F32), jax.ShapeDtypeStruct((T, LANES), jnp.int32),
                   jax.ShapeDtypeStruct((T, LANES), _F32)],
        compiler_params=_params(("parallel",)),
        name="ffn_norm_router",
    )(h, g.reshape(1, D), wr)


MOE_ROWS = 256


def _expert_body(blk_e_ref, nused_ref, gsrc_ref, sdst_ref, f_hbm, w_ref, wg_ref, wu_ref, wd_ref, y_hbm,
                 xbuf, obuf, sem_in, sem_out):
    i = pl.program_id(0)
    nused = nused_ref[0]
    slot = i % 2
    other = 1 - slot

    def gather(blk, s, r):
        return pltpu.make_async_copy(f_hbm.at[pl.ds(gsrc_ref[blk * MOE_ROWS + r], 1)], xbuf.at[s, pl.ds(r, 1)],
                                     sem_in.at[s])

    def scatter(blk1, s, r):
        return pltpu.make_async_copy(obuf.at[s, pl.ds(r, 1)], y_hbm.at[pl.ds(sdst_ref[blk1 * MOE_ROWS + r], 1)],
                                     sem_out.at[s])

    def wait_gather(s):
        pltpu.make_async_copy(f_hbm.at[pl.ds(0, MOE_ROWS)], xbuf.at[s], sem_in.at[s]).wait()

    def wait_scatter(s):
        pltpu.make_async_copy(obuf.at[s], y_hbm.at[pl.ds(0, MOE_ROWS)], sem_out.at[s]).wait()

    @pl.when(i == 0)
    def _():
        obuf[...] = jnp.zeros(obuf.shape, _F32)
        spare0 = pltpu.make_async_copy(obuf.at[0], y_hbm.at[pl.ds(y_hbm.shape[0] - 2 * MOE_ROWS, MOE_ROWS)],
                                       sem_out.at[0])
        spare0.start()
        spare0.wait()

        def start(r, c):
            gather(0, 0, r).start()
            return c
        lax.fori_loop(0, MOE_ROWS, start, 0)

    @pl.when(i <= nused)
    def _():
        wait_gather(slot)

        @pl.when(i >= 1)
        def _():
            wait_scatter(slot)

        for r in range(MOE_ROWS):
            gather(i + 1, other, r).start()
            scatter(i, other, r).start()

        x = xbuf[slot].astype(_CD)
        hb = (jax.nn.silu(_dot(x, wg_ref[0])) * _dot(x, wu_ref[0])).astype(_CD)
        obuf[slot] = _dot(hb, wd_ref[0]) * w_ref[...]

        @pl.when(i == nused)
        def _():
            wait_scatter(other)
            wait_gather(other)


def _experts(f, blk_e, nused, gsrc, sdst, buf_w, wg, wu, wd):
    T, D = f.shape
    E, _, FF = wg.shape
    n_blocks = blk_e.shape[0]
    once = pl.Buffered(1)
    grid_spec = pltpu.PrefetchScalarGridSpec(
        num_scalar_prefetch=4,
        grid=(n_blocks,),
        in_specs=[pl.BlockSpec(memory_space=pl.ANY),
                  pl.BlockSpec((MOE_ROWS, 1), lambda i, be, nu, gs, sd: (i, 0)),
                  pl.BlockSpec((1, D, FF), lambda i, be, nu, gs, sd: (be[i], 0, 0), pipeline_mode=once),
                  pl.BlockSpec((1, D, FF), lambda i, be, nu, gs, sd: (be[i], 0, 0), pipeline_mode=once),
                  pl.BlockSpec((1, FF, D), lambda i, be, nu, gs, sd: (be[i], 0, 0), pipeline_mode=once)],
        out_specs=pl.BlockSpec(memory_space=pl.ANY),
        scratch_shapes=[pltpu.VMEM((2, MOE_ROWS, D), _F32), pltpu.VMEM((2, MOE_ROWS, D), _F32),
                        pltpu.SemaphoreType.DMA((2,)), pltpu.SemaphoreType.DMA((2,))],
    )
    return pl.pallas_call(
        _expert_body,
        grid_spec=grid_spec,
        out_shape=jax.ShapeDtypeStruct((MOE_TOPK * T + 2 * MOE_ROWS, D), _F32),
        compiler_params=_params(("arbitrary",), VMEM_LIMIT_EXPERTS),
        name="moe_experts",
    )(blk_e, nused, gsrc, sdst, f, buf_w, wg, wu, wd)


def _group_by_expert(eid, wts):
    T = eid.shape[0]
    n_assign = T * MOE_TOPK
    n_blocks = -(-n_assign // MOE_ROWS) + MOE_EXPERTS
    e_flat = eid.reshape(-1)
    order = jnp.argsort(e_flat, stable=True).astype(jnp.int32)
    counts = jnp.bincount(e_flat, length=MOE_EXPERTS).astype(jnp.int32)
    starts = jnp.cumsum(counts) - counts
    pcounts = (counts + MOE_ROWS - 1) // MOE_ROWS * MOE_ROWS
    pends = jnp.cumsum(pcounts)
    pstarts = pends - pcounts
    blk_start = jnp.arange(n_blocks, dtype=jnp.int32) * MOE_ROWS
    blk_e = jnp.minimum(jnp.sum(pends[None, :] <= blk_start[:, None], axis=1), MOE_EXPERTS - 1).astype(jnp.int32)
    slot = jnp.arange(n_blocks * MOE_ROWS, dtype=jnp.int32)
    e_p = jnp.repeat(blk_e, MOE_ROWS)
    off = slot - pstarts[e_p]
    valid = (off >= 0) & (off < counts[e_p])
    src = order[jnp.clip(starts[e_p] + off, 0, n_assign - 1)]
    buf_w = jnp.where(valid, wts.reshape(-1)[src], 0.0)
    spare = n_assign + ((slot // MOE_ROWS) % 2) * MOE_ROWS + slot % MOE_ROWS
    gsrc = jnp.where(valid, src // MOE_TOPK, 0)
    sdst = jnp.where(valid, (src % MOE_TOPK) * T + src // MOE_TOPK, spare)
    gsrc = jnp.concatenate([gsrc, jnp.zeros((MOE_ROWS,), jnp.int32)]).astype(jnp.int32)
    sdst = jnp.concatenate([n_assign + MOE_ROWS + jnp.arange(MOE_ROWS, dtype=jnp.int32), sdst]).astype(jnp.int32)
    nused = (pends[-1:] // MOE_ROWS).astype(jnp.int32)
    return blk_e, nused, gsrc, sdst, buf_w.reshape(-1, 1)


def _addnorm_body(h_ref, y0_ref, y1_ref, g_ref, h2_ref, n_ref):
    h2 = h_ref[...] + (y0_ref[...] + y1_ref[...])
    h2_ref[...] = h2
    y = h2 * lax.rsqrt(jnp.mean(h2 * h2, axis=-1, keepdims=True) + NORM_EPS)
    n_ref[...] = (y * g_ref[...]).astype(n_ref.dtype)


def _add_norm(h, y2, g, tm=256):
    T, D = h.shape
    nt = T // tm
    return pl.pallas_call(
        _addnorm_body,
        grid=(nt,),
        in_specs=[pl.BlockSpec((tm, D), lambda i: (i, 0)),
                  pl.BlockSpec((tm, D), lambda i: (i, 0)),
                  pl.BlockSpec((tm, D), lambda i: (nt + i, 0)),
                  pl.BlockSpec((1, D), lambda i: (0, 0))],
        out_specs=[pl.BlockSpec((tm, D), lambda i: (i, 0)), pl.BlockSpec((tm, D), lambda i: (i, 0))],
        out_shape=[jax.ShapeDtypeStruct((T, D), _F32), jax.ShapeDtypeStruct((T, D), _CD)],
        compiler_params=_params(("parallel",)),
        name="moe_add_norm",
    )(h, y2, y2, g.reshape(1, D))


def kernel(x, p, attn_norm, w_in, nsa_cmp_pe, nsa_cmp_w1, nsa_cmp_w2, w_up_nsa, w_up_moba, w_out, ffn_norm, router_group, router_expert, expert_w_gate, expert_w_up, expert_w_down, ple_norm, ple_gate, ple_proj, final_norm):
    B, S, D = x.shape
    depth = w_in.shape[0]
    T = B * S
    tm = min(1024, T)
    tm_in = min(1024, S)
    tn = min(512, D)
    assert D % tn == 0 and T % tm == 0 and S % tm_in == 0

    o_kv = NSA_Q_W
    o_gate = o_kv + 6 * NSA_KV_W
    o_qkvb = o_gate + NSA_GATE_W
    o_merge = o_qkvb + 3 * MOBA_W
    col_gate = COL_MERGE + 2 * D
    n_rest = col_gate + GATE_PAD

    cos, sin = _rope_tables(jnp.arange(S))
    ccos, csin = _rope_tables(jnp.arange(S // NSA_CMP_STRIDE) * NSA_CMP_STRIDE + NSA_CMP_LEN - 1)

    h = x.reshape(T, D)
    for i in range(depth):
        wi = w_in[i]
        w_heads = jnp.concatenate([wi[:, 0:o_kv], wi[:, o_kv + 2 * NSA_KV_W:o_gate], wi[:, o_qkvb:o_merge]],
                                  axis=1).astype(_CD)
        w_rest = jnp.concatenate([wi[:, o_kv:o_kv + 2 * NSA_KV_W], wi[:, o_merge:o_merge + 2 * D],
                                  wi[:, o_gate:o_qkvb], jnp.zeros((D, GATE_PAD - NSA_GATE_W), wi.dtype)],
                                 axis=1).astype(_CD)
        a = _rmsnorm(h, attn_norm[i], _CD)
        zb3 = _head_proj(a, w_heads, cos, sin, S, tm_in).reshape(B, S, HEADS_W)
        zr = _mm_call(_mm_plain_body, [a], [w_rest], [], n_rest, _F32, tm_in, tn, "in_proj_rest")
        zr3 = zr.reshape(B, S, n_rest)
        kvc = _nsa_compress(zr3, nsa_cmp_pe[i], nsa_cmp_w1[i], nsa_cmp_w2[i], ccos, csin)
        o_a = _nsa_attention(zb3, kvc, zr3, col_gate, S).reshape(T, NSA_Q_W)
        o_b = _moba_attention(zb3, S).reshape(T, MOBA_W)
        mixed = _mm_call(_mm_gated_body, [o_a, o_b], [w_up_nsa[i].astype(_CD), w_up_moba[i].astype(_CD)],
                         [(zr, COL_MERGE), (zr, COL_MERGE + D)], D, _CD, tm, tn, "merge_up")
        h = _mm_call(_mm_resid_body, [mixed], [w_out[i].astype(_CD)], [(h, 0)], D, _F32, tm, tn, "out_proj")

        wr = jnp.concatenate([router_group[i], router_expert[i],
                              jnp.zeros((D, LANES - MOE_GROUPS - MOE_EXPERTS), _F32)], axis=1)
        f, eid, wts = _norm_router(h, ffn_norm[i], wr)
        blk_e, nused, gsrc, sdst, buf_w = _group_by_expert(eid[:, :MOE_TOPK], wts[:, :MOE_TOPK])
        y2 = _experts(f, blk_e, nused, gsrc, sdst, buf_w, expert_w_gate[i].astype(_CD),
                      expert_w_up[i].astype(_CD), expert_w_down[i].astype(_CD))
        h2, n = _add_norm(h, y2, ple_norm[i])
        h = _mm_call(_mm_ple_body, [n, p[i].reshape(T, -1)], [ple_gate[i].astype(_CD), ple_proj[i].astype(_CD)],
                     [(h2, 0)], D, _F32, tm, tn, "ple")
    return _rmsnorm(h, final_norm, _F32).reshape(B, S, D)
```

```python
import functools

import numpy as np
import jax
import jax.numpy as jnp
from jax import lax
from jax.experimental import pallas as pl
from jax.experimental.pallas import tpu as pltpu

HEAD_DIM = 128
ROPE_THETA = 10000.0
NORM_EPS = 1e-6
NEG_INF = -1e30
FORCE_SCORE = 1e9
SCALE = HEAD_DIM ** -0.5

NSA_Q_HEADS = 16
NSA_KV_HEADS = 4
NSA_HPG = NSA_Q_HEADS // NSA_KV_HEADS
NSA_CMP_LEN = 32
NSA_CMP_STRIDE = 16
NSA_SEL_BLOCK = 64
NSA_TOP_N = 16
NSA_WINDOW = 512
NSA_Q_W = NSA_Q_HEADS * HEAD_DIM
NSA_KV_W = NSA_KV_HEADS * HEAD_DIM
NSA_GATE_W = NSA_Q_HEADS * 3

MOBA_HEADS = 16
MOBA_BLOCK = 256
MOBA_TOPK = 3
MOBA_W = MOBA_HEADS * HEAD_DIM

MOE_GROUPS = 4
MOE_EXPERTS_PER_GROUP = 8
MOE_EXPERTS = MOE_GROUPS * MOE_EXPERTS_PER_GROUP
MOE_TOPK = 2

LANES = 128
SEL_PAD = 128
GATE_PAD = 512
VMEM_LIMIT = 48 * 1024 * 1024
VMEM_LIMIT_EXPERTS = 56 * 1024 * 1024

COL_QA = 0
COL_KS = COL_QA + NSA_Q_W
COL_VS = COL_KS + NSA_KV_W
COL_KW = COL_VS + NSA_KV_W
COL_VW = COL_KW + NSA_KV_W
COL_QB = COL_VW + NSA_KV_W
COL_KB = COL_QB + MOBA_W
COL_VB = COL_KB + MOBA_W
HEADS_W = COL_VB + MOBA_W
COL_KC = 0
COL_VC = COL_KC + NSA_KV_W
COL_MERGE = COL_VC + NSA_KV_W

_CD = jnp.bfloat16
_F32 = jnp.float32


def _dot(a, b):
    return jnp.dot(a, b, preferred_element_type=_F32)


def _split(x):
    hi = x.astype(_CD)
    lo = (x - hi.astype(_F32)).astype(_CD)
    return hi, lo


def _t(x):
    return jnp.transpose(x.astype(_F32))


def _params(sem, vmem=VMEM_LIMIT):
    return pltpu.CompilerParams(dimension_semantics=sem, vmem_limit_bytes=vmem)


def _rmsnorm_body(x_ref, g_ref, o_ref):
    x = x_ref[...]
    y = x * lax.rsqrt(jnp.mean(x * x, axis=-1, keepdims=True) + NORM_EPS)
    o_ref[...] = (y * g_ref[...]).astype(o_ref.dtype)


def _rmsnorm(x, g, out_dtype, tm=256):
    T, D = x.shape
    return pl.pallas_call(
        _rmsnorm_body,
        grid=(T // tm,),
        in_specs=[pl.BlockSpec((tm, D), lambda i: (i, 0)),
                  pl.BlockSpec((1, D), lambda i: (0, 0))],
        out_specs=pl.BlockSpec((tm, D), lambda i: (i, 0)),
        out_shape=jax.ShapeDtypeStruct((T, D), out_dtype),
        compiler_params=_params(("parallel",)),
        name="rmsnorm",
    )(x, g.reshape(1, D))


def _mm_call(body, lhs, rhs, extras, n_out, out_dtype, tm, tn, name):
    M = lhs[0].shape[0]
    in_specs = [pl.BlockSpec((tm, a.shape[1]), lambda i, j: (i, 0)) for a in lhs]
    in_specs += [pl.BlockSpec((w.shape[0], tn), lambda i, j: (0, j)) for w in rhs]
    args = list(lhs) + list(rhs)
    for arr, off in extras:
        assert off % tn == 0
        in_specs.append(pl.BlockSpec((tm, tn), functools.partial(lambda i, j, o: (i, j + o), o=off // tn)))
        args.append(arr)
    return pl.pallas_call(
        body,
        grid=(M // tm, n_out // tn),
        in_specs=in_specs,
        out_specs=pl.BlockSpec((tm, tn), lambda i, j: (i, j)),
        out_shape=jax.ShapeDtypeStruct((M, n_out), out_dtype),
        compiler_params=_params(("parallel", "parallel")),
        name=name,
    )(*args)


def _mm_plain_body(a_ref, w_ref, o_ref):
    o_ref[...] = _dot(a_ref[...], w_ref[...]).astype(o_ref.dtype)


def _mm_gated_body(a1_ref, a2_ref, w1_ref, w2_ref, m1_ref, m2_ref, o_ref):
    y1 = _dot(a1_ref[...], w1_ref[...])
    y2 = _dot(a2_ref[...], w2_ref[...])
    o_ref[...] = (jax.nn.sigmoid(m1_ref[...]) * y1 + jax.nn.sigmoid(m2_ref[...]) * y2).astype(o_ref.dtype)


def _mm_resid_body(a_ref, w_ref, r_ref, o_ref):
    o_ref[...] = r_ref[...] + _dot(a_ref[...], w_ref[...])


def _mm_ple_body(n_ref, p_ref, wg_ref, wp_ref, r_ref, o_ref):
    gate = jax.nn.sigmoid(_dot(n_ref[...], wg_ref[...]))
    emb = _dot(p_ref[...].astype(_CD), wp_ref[...])
    o_ref[...] = r_ref[...] + gate * emb


def _rope_tables(pos):
    half = HEAD_DIM // 2
    inv = jnp.power(ROPE_THETA, -jnp.arange(half, dtype=_F32) / half)
    ang = pos.astype(_F32)[:, None] * inv
    c, s = jnp.cos(ang), jnp.sin(ang)
    return jnp.concatenate([c, c], axis=-1), jnp.concatenate([-s, s], axis=-1)


def _rope(x, c, s):
    return x * c + pltpu.roll(x, HEAD_DIM // 2, 1) * s


HEADS_TN = 512
_ROPE_BLOCKS = ((COL_QA // HEADS_TN, COL_KS // HEADS_TN), (COL_KS // HEADS_TN, COL_VS // HEADS_TN),
                (COL_KW // HEADS_TN, COL_VW // HEADS_TN), (COL_QB // HEADS_TN, COL_VB // HEADS_TN))


def _mm_heads_body(a_ref, w_ref, cos_ref, sin_ref, o_ref):
    j = pl.program_id(1)
    y = _dot(a_ref[...], w_ref[...])
    is_rope = functools.reduce(jnp.logical_or, [(j >= a) & (j < b) for a, b in _ROPE_BLOCKS])

    @pl.when(is_rope)
    def _():
        c = cos_ref[...]
        s = sin_ref[...]
        for h in range(HEADS_TN // HEAD_DIM):
            sl = slice(h * HEAD_DIM, (h + 1) * HEAD_DIM)
            o_ref[:, sl] = _rope(y[:, sl], c, s).astype(o_ref.dtype)

    @pl.when(jnp.logical_not(is_rope))
    def _():
        o_ref[...] = y.astype(o_ref.dtype)


def _head_proj(a, w, cos, sin, S, tm):
    T, D = a.shape
    nps = S // tm
    return pl.pallas_call(
        _mm_heads_body,
        grid=(T // tm, HEADS_W // HEADS_TN),
        in_specs=[pl.BlockSpec((tm, D), lambda i, j: (i, 0)),
                  pl.BlockSpec((D, HEADS_TN), lambda i, j: (0, j)),
                  pl.BlockSpec((tm, HEAD_DIM), lambda i, j: (i % nps, 0)),
                  pl.BlockSpec((tm, HEAD_DIM), lambda i, j: (i % nps, 0))],
        out_specs=pl.BlockSpec((tm, HEADS_TN), lambda i, j: (i, j)),
        out_shape=jax.ShapeDtypeStruct((T, HEADS_W), _CD),
        compiler_params=_params(("parallel", "parallel")),
        name="in_proj_heads",
    )(a, w, cos, sin)


def _compress_body(x_ref, pe_ref, w1_ref, w2_ref, cos_ref, sin_ref, o_ref, xpad_ref, *, S, nc):
    ncp = S // NSA_CMP_STRIDE
    xpad_ref[0:S, :] = x_ref[0]
    xpad_ref[S:S + NSA_CMP_LEN, :] = jnp.zeros((NSA_CMP_LEN, HEAD_DIM), _F32)
    acc = jnp.zeros((ncp, HEAD_DIM), _F32)
    for l in range(NSA_CMP_LEN):
        xl = xpad_ref[pl.ds(l, ncp, stride=NSA_CMP_STRIDE), :] + pe_ref[0, l:l + 1, :]
        acc = acc + _dot(xl.astype(_CD), w1_ref[0, l].astype(_CD))
    hid = jax.nn.gelu(acc)
    out = _dot(hid.astype(_CD), w2_ref[0].astype(_CD))
    is_key = pl.program_id(1) == 0
    out = jnp.where(is_key, _rope(out, cos_ref[...], sin_ref[...]), out)
    row = lax.broadcasted_iota(jnp.int32, (ncp, HEAD_DIM), 0)
    o_ref[0, 0, 0] = jnp.where(row < nc, out, 0.0).astype(o_ref.dtype)


def _nsa_compress(zr3, pe, w1, w2, ccos, csin):
    B, S, _ = zr3.shape
    G = NSA_KV_HEADS
    ncp = S // NSA_CMP_STRIDE
    nc = (S - NSA_CMP_LEN) // NSA_CMP_STRIDE + 1
    cb = COL_KC // HEAD_DIM
    return pl.pallas_call(
        functools.partial(_compress_body, S=S, nc=nc),
        grid=(B, 2, G),
        in_specs=[pl.BlockSpec((1, S, HEAD_DIM), lambda b, t, g: (b, 0, cb + t * G + g)),
                  pl.BlockSpec((1, NSA_CMP_LEN, HEAD_DIM), lambda b, t, g: (t, 0, 0)),
                  pl.BlockSpec((1, NSA_CMP_LEN, HEAD_DIM, HEAD_DIM), lambda b, t, g: (t, 0, 0, 0)),
                  pl.BlockSpec((1, HEAD_DIM, HEAD_DIM), lambda b, t, g: (t, 0, 0)),
                  pl.BlockSpec((ncp, HEAD_DIM), lambda b, t, g: (0, 0)),
                  pl.BlockSpec((ncp, HEAD_DIM), lambda b, t, g: (0, 0))],
        out_specs=pl.BlockSpec((1, 1, 1, ncp, HEAD_DIM), lambda b, t, g: (b, t, g, 0, 0)),
        out_shape=jax.ShapeDtypeStruct((B, 2, G, ncp, HEAD_DIM), _CD),
        scratch_shapes=[pltpu.VMEM((S + NSA_CMP_LEN, HEAD_DIM), _F32)],
        compiler_params=_params(("parallel", "parallel", "parallel")),
        name="nsa_compress",
    )(zr3, pe, w1, w2, ccos, csin)


SCALE_LOG2 = SCALE * 1.4426950408889634


def _online_step_t(s2, vt, m, l, acc_ref, allowed=None):
    m_new = jnp.maximum(m, jnp.max(s2, axis=0, keepdims=True))
    alpha = jnp.exp2(m - m_new)
    p = jnp.exp2(s2 - m_new)
    if allowed is not None:
        p = jnp.where(allowed, p, 0.0)
    l = alpha * l + jnp.sum(p, axis=0, keepdims=True)
    acc_ref[...] = alpha * acc_ref[...] + _dot(vt, p.astype(_CD))
    return m_new, l


def _next_below(x):
    bits = pltpu.bitcast(x, jnp.int32)
    stepped = pltpu.bitcast(jnp.where(bits > 0, bits - 1, bits + 1), _F32)
    return jnp.where(x == 0.0, -1e-30, stepped)


def _rank_rows(val_ref, thr_ref, n_rows):
    thr_ref[...] = _next_below(val_ref[...])

    def step(mm, cnt):
        for u in range(RANK_UNROLL):
            m = mm * RANK_UNROLL + u
            row = val_ref[pl.ds(m, 1), :]
            thr_ref[pl.ds(m, 1), :] = row
            cnt = cnt + jnp.where(row > thr_ref[...], 1.0, 0.0)
        return cnt

    return lax.fori_loop(0, (n_rows + RANK_UNROLL - 1) // RANK_UNROLL, step, jnp.zeros(val_ref.shape, _F32))


def _bias_rows(ref, first, count, rows_each, width):
    return jnp.concatenate([jnp.broadcast_to(ref[pl.ds(first + u, 1), :], (rows_each, width)) for u in range(count)],
                           axis=0)


NSA_TQ = 256
NSA_TK = 1024
RANK_UNROLL = 4


def _nsa_body(q_ref, ks_ref, vs_ref, kw_ref, vw_ref, kc_ref, vc_ref, gate_ref, ovt_ref, o_ref,
              vst_ref, vwt_ref, vct_ref, imp_ref, thr_ref, selb_ref, gt_ref, acc_ref, out_ref, *, S, gate_lane0):
    TQ, TK, H = NSA_TQ, NSA_TK, NSA_HPG
    R = H * TQ
    g = pl.program_id(1)
    qi = pl.program_id(2)
    t0 = qi * TQ
    ncp = S // NSA_CMP_STRIDE

    @pl.when(qi == 0)
    def _():
        def tr(n, c):
            for u in range(TK // LANES):
                r0 = pl.multiple_of(n * TK + u * LANES, LANES)
                cols = slice(u * LANES, (u + 1) * LANES)
                vst_ref[n, :, cols] = _t(vs_ref[0, pl.ds(r0, LANES), :]).astype(_CD)
                vwt_ref[n * (TK // LANES) + u] = _t(vw_ref[0, pl.ds(r0, LANES), :]).astype(_CD)
            return c
        lax.fori_loop(0, S // TK, tr, 0)
        for n in range(ncp // LANES):
            cols = slice(n * LANES, (n + 1) * LANES)
            vct_ref[:, cols] = _t(vc_ref[0, 0, 0, cols, :]).astype(_CD)

    q = q_ref[0]
    qt = jnp.concatenate([_t(q[:, h * HEAD_DIM:(h + 1) * HEAD_DIM]).astype(_CD) for h in range(H)], axis=1)
    pos_l = t0 + (lax.broadcasted_iota(jnp.int32, (1, R), 1) & (TQ - 1))

    gt_ref[...] = jnp.transpose(jax.nn.sigmoid(gate_ref[0]))

    def gate_row(c):
        return jnp.concatenate([gt_ref[pl.ds(gate_lane0 + (g * H + h) * 3 + c, 1), :] for h in range(H)], axis=1)

    cmp_end = lax.broadcasted_iota(jnp.int32, (ncp, 1), 0) * NSA_CMP_STRIDE + (NSA_CMP_LEN - 1)
    s_c = jnp.where(cmp_end <= pos_l, _dot(kc_ref[0, 0, 0], qt) * SCALE_LOG2, NEG_INF)
    p = jnp.exp2(s_c - jnp.max(s_c, axis=0, keepdims=True))
    inv = jnp.where(pos_l >= NSA_CMP_LEN - 1, 1.0 / jnp.sum(p, axis=0, keepdims=True), 0.0)
    p_c = p * inv
    out_ref[...] = gate_row(0) * _dot(vct_ref[...], p_c.astype(_CD))

    p_g = p_c[:, 0:TQ]
    for h in range(1, H):
        p_g = p_g + p_c[:, h * TQ:(h + 1) * TQ]
    p_hi, p_lo = _split(p_g)
    ovt = ovt_ref[...]
    imp = _dot(ovt, p_hi) + _dot(ovt, p_lo)
    j_io = lax.broadcasted_iota(jnp.int32, (SEL_PAD, TQ), 0)
    cur = (t0 + lax.broadcasted_iota(jnp.int32, (SEL_PAD, TQ), 1)) // NSA_SEL_BLOCK
    forced = (j_io == 0) | (j_io == cur) | (j_io == cur - 1)
    imp = jnp.where(j_io <= cur, jnp.where(forced, FORCE_SCORE, imp), NEG_INF)
    imp_ref[...] = imp

    rank = _rank_rows(imp_ref, thr_ref, (t0 + TQ - 1) // NSA_SEL_BLOCK + 1)
    sel_bias = jnp.where(rank < float(NSA_TOP_N), 0.0, NEG_INF)
    selb_ref[...] = jnp.concatenate([sel_bias] * H, axis=1)

    key_io = lax.broadcasted_iota(jnp.int32, (TK, 1), 0)
    init = (jnp.full((1, R), NEG_INF, _F32), jnp.zeros((1, R), _F32))
    spb = TK // NSA_SEL_BLOCK
    n_full = t0 // TK

    def sel_scores(kb):
        k0 = pl.multiple_of(kb * TK, TK)
        s2 = _dot(ks_ref[0, pl.ds(k0, TK), :], qt) * SCALE_LOG2
        return s2 + _bias_rows(selb_ref, kb * spb, spb, NSA_SEL_BLOCK, R)

    acc_ref[...] = jnp.zeros(acc_ref.shape, _F32)
    causal = jnp.where((n_full * TK + key_io) <= pos_l, 0.0, NEG_INF)
    carry = _online_step_t(sel_scores(n_full) + causal, vst_ref[n_full], *init, acc_ref)
    _, l_s = lax.fori_loop(0, n_full, lambda kb, c: _online_step_t(sel_scores(kb), vst_ref[kb], *c, acc_ref), carry)
    out_ref[...] += (gate_row(1) * (1.0 / l_s)) * acc_ref[...]

    wk = NSA_WINDOW + TQ
    w0 = pl.multiple_of(jnp.maximum(t0 - NSA_WINDOW, 0), TQ)
    dist = pos_l - (w0 + lax.broadcasted_iota(jnp.int32, (wk, 1), 0))
    in_win = (dist >= 0) & (dist < NSA_WINDOW)
    s_w = jnp.where(in_win, _dot(kw_ref[0, pl.ds(w0, wk), :], qt) * SCALE_LOG2, NEG_INF)
    p_w = jnp.exp2(s_w - jnp.max(s_w, axis=0, keepdims=True))
    l_w = jnp.sum(p_w, axis=0, keepdims=True)
    p_w = p_w.astype(_CD)
    acc_w = _dot(vwt_ref[w0 // LANES], p_w[0:LANES])
    for u in range(1, wk // LANES):
        acc_w = acc_w + _dot(vwt_ref[w0 // LANES + u], p_w[u * LANES:(u + 1) * LANES])
    out_t = out_ref[...] + (gate_row(2) * (1.0 / l_w)) * acc_w

    for h in range(H):
        o_ref[0, :, h * HEAD_DIM:(h + 1) * HEAD_DIM] = jnp.transpose(out_t[:, h * TQ:(h + 1) * TQ]).astype(o_ref.dtype)


def _overlap_t(S):
    ncp = S // NSA_CMP_STRIDE
    nc = (S - NSA_CMP_LEN) // NSA_CMP_STRIDE + 1
    ns = S // NSA_SEL_BLOCK
    cs = np.arange(nc)[:, None] * NSA_CMP_STRIDE
    ss = np.arange(ns)[None, :] * NSA_SEL_BLOCK
    ov = np.clip(np.minimum(cs + NSA_CMP_LEN, ss + NSA_SEL_BLOCK) - np.maximum(cs, ss), 0, None) / NSA_CMP_LEN
    out = np.zeros((SEL_PAD, ncp), np.float32)
    out[:ns, :nc] = ov.T
    return out


def _nsa_attention(zb3, kvc, zr3, gate_col, S):
    B = zb3.shape[0]
    G, TQ, TK = NSA_KV_HEADS, NSA_TQ, NSA_TK
    assert S // NSA_SEL_BLOCK <= SEL_PAD and S % TK == 0 and TK % TQ == 0 and S >= NSA_WINDOW + TQ
    ncp = S // NSA_CMP_STRIDE
    assert ncp % LANES == 0
    qw = NSA_HPG * HEAD_DIM
    R = NSA_HPG * TQ
    ovt = jnp.asarray(_overlap_t(S)).astype(_CD)
    resident = lambda col: pl.BlockSpec((1, S, HEAD_DIM), lambda b, g, i: (b, 0, col // HEAD_DIM + g))
    return pl.pallas_call(
        functools.partial(_nsa_body, S=S, gate_lane0=gate_col % LANES),
        grid=(B, G, S // TQ),
        in_specs=[pl.BlockSpec((1, TQ, qw), lambda b, g, i: (b, i, g)),
                  resident(COL_KS), resident(COL_VS), resident(COL_KW), resident(COL_VW),
                  pl.BlockSpec((1, 1, 1, ncp, HEAD_DIM), lambda b, g, i: (b, 0, g, 0, 0)),
                  pl.BlockSpec((1, 1, 1, ncp, HEAD_DIM), lambda b, g, i: (b, 1, g, 0, 0)),
                  pl.BlockSpec((1, TQ, LANES), lambda b, g, i: (b, i, gate_col // LANES)),
                  pl.BlockSpec((SEL_PAD, ncp), lambda b, g, i: (0, 0))],
        out_specs=pl.BlockSpec((1, TQ, qw), lambda b, g, i: (b, i, g)),
        out_shape=jax.ShapeDtypeStruct((B, S, NSA_Q_W), _CD),
        scratch_shapes=[pltpu.VMEM((S // TK, HEAD_DIM, TK), _CD),
                        pltpu.VMEM((S // LANES, HEAD_DIM, LANES), _CD),
                        pltpu.VMEM((HEAD_DIM, ncp), _CD),
                        pltpu.VMEM((SEL_PAD, TQ), _F32),
                        pltpu.VMEM((SEL_PAD, TQ), _F32),
                        pltpu.VMEM((SEL_PAD, R), _F32),
                        pltpu.VMEM((LANES, TQ), _F32),
                        pltpu.VMEM((HEAD_DIM, R), _F32),
                        pltpu.VMEM((HEAD_DIM, R), _F32)],
        compiler_params=_params(("parallel", "parallel", "arbitrary")),
        name="nsa_attention",
    )(zb3, zb3, zb3, zb3, zb3, kvc, kvc, zr3, ovt)


MOBA_NB = 4
MOBA_TQ = 1024


def _moba_body(q_ref, k_ref, v_ref, o_ref, kmean_ref, gate_ref, thr_ref, selb_ref, vt_ref, acc_ref, *, S):
    TQ, NB = MOBA_TQ, MOBA_NB
    TK = NB * MOBA_BLOCK
    nf = S // MOBA_BLOCK
    nfp = kmean_ref.shape[0]
    qi = pl.program_id(2)
    t0 = pl.multiple_of(qi * TQ, TQ)

    @pl.when(qi == 0)
    def _():
        kmean_ref[...] = jnp.zeros(kmean_ref.shape, _F32)
        for n in range(nf):
            rows = slice(n * MOBA_BLOCK, (n + 1) * MOBA_BLOCK)
            kmean_ref[n:n + 1, :] = jnp.mean(k_ref[0, rows, :].astype(_F32), axis=0, keepdims=True)
            cols = slice((n % NB) * MOBA_BLOCK, (n % NB + 1) * MOBA_BLOCK)
            vt_ref[n // NB, :, cols] = _t(v_ref[0, rows, :]).astype(_CD)

    qt = _t(q_ref[0]).astype(_CD)
    km_hi, km_lo = _split(kmean_ref[...])
    gate = _dot(km_hi, qt) + _dot(km_lo, qt)
    n_io = lax.broadcasted_iota(jnp.int32, (nfp, TQ), 0)
    qb = (t0 + lax.broadcasted_iota(jnp.int32, (nfp, TQ), 1)) // MOBA_BLOCK
    gate = jnp.where(n_io < qb, gate, NEG_INF)
    gate_ref[...] = gate

    rank = _rank_rows(gate_ref, thr_ref, (t0 + TQ - 1) // MOBA_BLOCK)
    keep = ((rank < float(MOBA_TOPK)) & (n_io < qb)) | (n_io == qb)
    selb_ref[...] = jnp.where(keep, 0.0, NEG_INF)

    pos_l = t0 + lax.broadcasted_iota(jnp.int32, (1, TQ), 1)
    key_io = lax.broadcasted_iota(jnp.int32, (TK, 1), 0)
    init = (jnp.full((1, TQ), NEG_INF, _F32), jnp.zeros((1, TQ), _F32))
    n_full = t0 // TK

    def scores(kb):
        k0 = pl.multiple_of(kb * TK, TK)
        s2 = _dot(k_ref[0, pl.ds(k0, TK), :], qt) * SCALE_LOG2
        return s2 + _bias_rows(selb_ref, kb * NB, NB, MOBA_BLOCK, TQ)

    acc_ref[...] = jnp.zeros(acc_ref.shape, _F32)
    causal = jnp.where((n_full * TK + key_io) <= pos_l, 0.0, NEG_INF)
    carry = _online_step_t(scores(n_full) + causal, vt_ref[n_full], *init, acc_ref)
    _, l = lax.fori_loop(0, n_full, lambda kb, c: _online_step_t(scores(kb), vt_ref[kb], *c, acc_ref), carry)
    o_ref[0] = jnp.transpose(acc_ref[...] * (1.0 / l)).astype(o_ref.dtype)


def _moba_attention(zb3, S):
    B = zb3.shape[0]
    H, TQ = MOBA_HEADS, MOBA_TQ
    assert S % (MOBA_NB * MOBA_BLOCK) == 0 and (MOBA_NB * MOBA_BLOCK) % TQ == 0 and TQ % MOBA_BLOCK == 0
    nf = S // MOBA_BLOCK
    nfp = -(-nf // 8) * 8
    assert nfp % RANK_UNROLL == 0
    tile = lambda col: pl.BlockSpec((1, TQ, HEAD_DIM), lambda b, h, i: (b, i, col // HEAD_DIM + h))
    resident = lambda col: pl.BlockSpec((1, S, HEAD_DIM), lambda b, h, i: (b, 0, col // HEAD_DIM + h))
    return pl.pallas_call(
        functools.partial(_moba_body, S=S),
        grid=(B, H, S // TQ),
        in_specs=[tile(COL_QB), resident(COL_KB), resident(COL_VB)],
        out_specs=pl.BlockSpec((1, TQ, HEAD_DIM), lambda b, h, i: (b, i, h)),
        out_shape=jax.ShapeDtypeStruct((B, S, MOBA_W), _CD),
        scratch_shapes=[pltpu.VMEM((nfp, HEAD_DIM), _F32),
                        pltpu.VMEM((nfp, TQ), _F32),
                        pltpu.VMEM((nfp, TQ), _F32),
                        pltpu.VMEM((nfp, TQ), _F32),
                        pltpu.VMEM((nf // MOBA_NB, HEAD_DIM, MOBA_NB * MOBA_BLOCK), _CD),
                        pltpu.VMEM((HEAD_DIM, TQ), _F32)],
        compiler_params=_params(("parallel", "parallel", "arbitrary")),
        name="moba_attention",
    )(zb3, zb3, zb3)


def _router_body(h_ref, g_ref, wr_ref, f_ref, eid_ref, wts_ref):
    x = h_ref[...]
    y = x * lax.rsqrt(jnp.mean(x * x, axis=-1, keepdims=True) + NORM_EPS)
    f = y * g_ref[...]
    f_ref[...] = f
    f_hi, f_lo = _split(f)
    w_hi, w_lo = _split(wr_ref[...])
    logits = _dot(f_hi, w_hi) + (_dot(f_hi, w_lo) + _dot(f_lo, w_hi))
    lane = lax.broadcasted_iota(jnp.int32, logits.shape, 1)
    big = jnp.int32(2 * LANES)
    in_g = lane < MOE_GROUPS
    lg = jnp.where(in_g, logits, -jnp.inf)
    mg = jnp.max(lg, axis=1, keepdims=True)
    pg_top = 1.0 / jnp.sum(jnp.where(in_g, jnp.exp(logits - mg), 0.0), axis=1, keepdims=True)
    grp = jnp.min(jnp.where(lg == mg, lane, big), axis=1, keepdims=True)
    eidx = lane - MOE_GROUPS
    in_e = (eidx >= 0) & (eidx < MOE_EXPERTS) & ((eidx // MOE_EXPERTS_PER_GROUP) == grp)
    le = jnp.where(in_e, logits, -jnp.inf)
    m1 = jnp.max(le, axis=1, keepdims=True)
    i1 = jnp.min(jnp.where(le == m1, lane, big), axis=1, keepdims=True)
    le2 = jnp.where(lane == i1, -jnp.inf, le)
    m2 = jnp.max(le2, axis=1, keepdims=True)
    i2 = jnp.min(jnp.where(le2 == m2, lane, big), axis=1, keepdims=True)
    e2 = jnp.exp(m2 - m1)
    w1 = pg_top / (1.0 + e2)
    w2 = pg_top * e2 / (1.0 + e2)
    eid_ref[...] = jnp.where(lane == 0, i1 - MOE_GROUPS, jnp.where(lane == 1, i2 - MOE_GROUPS, 0))
    wts_ref[...] = jnp.where(lane == 0, w1, jnp.where(lane == 1, w2, 0.0))


def _norm_router(h, g, wr, tm=256):
    T, D = h.shape
    row = lambda w: pl.BlockSpec((tm, w), lambda i: (i, 0))
    return pl.pallas_call(
        _router_body,
        grid=(T // tm,),
        in_specs=[row(D), pl.BlockSpec((1, D), lambda i: (0, 0)), pl.BlockSpec((D, LANES), lambda i: (0, 0))],
        out_specs=[row(D), row(LANES), row(LANES)],
        out_shape=[jax.ShapeDtypeStruct((T, D), _F32), jax.ShapeDtypeStruct((T, LANES), jnp.int32),
                   jax.ShapeDtypeStruct((T, LANES), _F32)],
        compiler_params=_params(("parallel",)),
        name="ffn_norm_router",
    )(h, g.reshape(1, D), wr)


MOE_ROWS = 256


def _expert_body(blk_e_ref, nused_ref, gsrc_ref, sdst_ref, f_hbm, w_ref, wg_ref, wu_ref, wd_ref, y_hbm,
                 xbuf, obuf, sem_in, sem_out):
    i = pl.program_id(0)
    nused = nused_ref[0]
    slot = i % 2
    other = 1 - slot

    def gather(blk, s, r):
        return pltpu.make_async_copy(f_hbm.at[pl.ds(gsrc_ref[blk * MOE_ROWS + r], 1)], xbuf.at[s, pl.ds(r, 1)],
                                     sem_in.at[s])

    def scatter(blk1, s, r):
        return pltpu.make_async_copy(obuf.at[s, pl.ds(r, 1)], y_hbm.at[pl.ds(sdst_ref[blk1 * MOE_ROWS + r], 1)],
                                     sem_out.at[s])

    def wait_gather(s):
        pltpu.make_async_copy(f_hbm.at[pl.ds(0, MOE_ROWS)], xbuf.at[s], sem_in.at[s]).wait()

    def wait_scatter(s):
        pltpu.make_async_copy(obuf.at[s], y_hbm.at[pl.ds(0, MOE_ROWS)], sem_out.at[s]).wait()

    @pl.when(i == 0)
    def _():
        obuf[...] = jnp.zeros(obuf.shape, _F32)
        spare0 = pltpu.make_async_copy(obuf.at[0], y_hbm.at[pl.ds(y_hbm.shape[0] - 2 * MOE_ROWS, MOE_ROWS)],
                                       sem_out.at[0])
        spare0.start()
        spare0.wait()

        def start(r, c):
            gather(0, 0, r).start()
            return c
        lax.fori_loop(0, MOE_ROWS, start, 0)

    @pl.when(i <= nused)
    def _():
        wait_gather(slot)

        @pl.when(i >= 1)
        def _():
            wait_scatter(slot)

        for r in range(MOE_ROWS):
            gather(i + 1, other, r).start()
            scatter(i, other, r).start()

        x = xbuf[slot].astype(_CD)
        hb = (jax.nn.silu(_dot(x, wg_ref[0, 0])) * _dot(x, wu_ref[0, 0])).astype(_CD)
        obuf[slot] = _dot(hb, wd_ref[0, 0]) * w_ref[...]

        @pl.when(i == nused)
        def _():
            wait_scatter(other)
            wait_gather(other)


def _experts(f, blk_e, nused, gsrc, sdst, buf_w, wg, wu, wd, layer):
    T, D = f.shape
    FF = wg.shape[-1]
    n_blocks = blk_e.shape[0]
    once = pl.Buffered(1)
    expert = lambda i, be, nu, gs, sd: (layer, be[i], 0, 0)
    grid_spec = pltpu.PrefetchScalarGridSpec(
        num_scalar_prefetch=4,
        grid=(n_blocks,),
        in_specs=[pl.BlockSpec(memory_space=pl.ANY),
                  pl.BlockSpec((MOE_ROWS, 1), lambda i, be, nu, gs, sd: (i, 0)),
                  pl.BlockSpec((1, 1, D, FF), expert, pipeline_mode=once),
                  pl.BlockSpec((1, 1, D, FF), expert, pipeline_mode=once),
                  pl.BlockSpec((1, 1, FF, D), expert, pipeline_mode=once)],
        out_specs=pl.BlockSpec(memory_space=pl.ANY),
        scratch_shapes=[pltpu.VMEM((2, MOE_ROWS, D), _F32), pltpu.VMEM((2, MOE_ROWS, D), _F32),
                        pltpu.SemaphoreType.DMA((2,)), pltpu.SemaphoreType.DMA((2,))],
    )
    return pl.pallas_call(
        _expert_body,
        grid_spec=grid_spec,
        out_shape=jax.ShapeDtypeStruct((MOE_TOPK * T + 2 * MOE_ROWS, D), _F32),
        compiler_params=_params(("arbitrary",), VMEM_LIMIT_EXPERTS),
        name="moe_experts",
    )(blk_e, nused, gsrc, sdst, f, buf_w, wg, wu, wd)


def _group_by_expert(eid, wts):
    T = eid.shape[0]
    n_assign = T * MOE_TOPK
    n_blocks = -(-n_assign // MOE_ROWS) + MOE_EXPERTS
    e_flat = eid.reshape(-1)
    order = jnp.argsort(e_flat, stable=True).astype(jnp.int32)
    counts = jnp.bincount(e_flat, length=MOE_EXPERTS).astype(jnp.int32)
    starts = jnp.cumsum(counts) - counts
    pcounts = (counts + MOE_ROWS - 1) // MOE_ROWS * MOE_ROWS
    pends = jnp.cumsum(pcounts)
    pstarts = pends - pcounts
    blk_start = jnp.arange(n_blocks, dtype=jnp.int32) * MOE_ROWS
    blk_e = jnp.minimum(jnp.sum(pends[None, :] <= blk_start[:, None], axis=1), MOE_EXPERTS - 1).astype(jnp.int32)
    slot = jnp.arange(n_blocks * MOE_ROWS, dtype=jnp.int32)
    per_slot = lambda table: jnp.repeat(table[blk_e], MOE_ROWS)
    off = slot - per_slot(pstarts)
    valid = (off >= 0) & (off < per_slot(counts))
    src = order[jnp.clip(per_slot(starts) + off, 0, n_assign - 1)]
    buf_w = jnp.where(valid, wts.reshape(-1)[src], 0.0)
    spare = n_assign + ((slot // MOE_ROWS) % 2) * MOE_ROWS + slot % MOE_ROWS
    gsrc = jnp.where(valid, src // MOE_TOPK, 0)
    sdst = jnp.where(valid, (src % MOE_TOPK) * T + src // MOE_TOPK, spare)
    gsrc = jnp.concatenate([gsrc, jnp.zeros((MOE_ROWS,), jnp.int32)]).astype(jnp.int32)
    sdst = jnp.concatenate([n_assign + MOE_ROWS + jnp.arange(MOE_ROWS, dtype=jnp.int32), sdst]).astype(jnp.int32)
    nused = (pends[-1:] // MOE_ROWS).astype(jnp.int32)
    return blk_e, nused, gsrc, sdst, buf_w.reshape(-1, 1)


def _addnorm_body(h_ref, y0_ref, y1_ref, g_ref, h2_ref, n_ref):
    h2 = h_ref[...] + (y0_ref[...] + y1_ref[...])
    h2_ref[...] = h2
    y = h2 * lax.rsqrt(jnp.mean(h2 * h2, axis=-1, keepdims=True) + NORM_EPS)
    n_ref[...] = (y * g_ref[...]).astype(n_ref.dtype)


def _add_norm(h, y2, g, tm=256):
    T, D = h.shape
    nt = T // tm
    return pl.pallas_call(
        _addnorm_body,
        grid=(nt,),
        in_specs=[pl.BlockSpec((tm, D), lambda i: (i, 0)),
                  pl.BlockSpec((tm, D), lambda i: (i, 0)),
                  pl.BlockSpec((tm, D), lambda i: (nt + i, 0)),
                  pl.BlockSpec((1, D), lambda i: (0, 0))],
        out_specs=[pl.BlockSpec((tm, D), lambda i: (i, 0)), pl.BlockSpec((tm, D), lambda i: (i, 0))],
        out_shape=[jax.ShapeDtypeStruct((T, D), _F32), jax.ShapeDtypeStruct((T, D), _CD)],
        compiler_params=_params(("parallel",)),
        name="moe_add_norm",
    )(h, y2, y2, g.reshape(1, D))


def kernel(x, p, attn_norm, w_in, nsa_cmp_pe, nsa_cmp_w1, nsa_cmp_w2, w_up_nsa, w_up_moba, w_out, ffn_norm, router_group, router_expert, expert_w_gate, expert_w_up, expert_w_down, ple_norm, ple_gate, ple_proj, final_norm):
    B, S, D = x.shape
    depth = w_in.shape[0]
    T = B * S
    tm = min(1024, T)
    tm_in = min(1024, S)
    tn = min(512, D)
    assert D % tn == 0 and T % tm == 0 and S % tm_in == 0

    o_kv = NSA_Q_W
    o_gate = o_kv + 6 * NSA_KV_W
    o_qkvb = o_gate + NSA_GATE_W
    o_merge = o_qkvb + 3 * MOBA_W
    col_gate = COL_MERGE + 2 * D
    n_rest = col_gate + GATE_PAD

    cos, sin = _rope_tables(jnp.arange(S))
    ccos, csin = _rope_tables(jnp.arange(S // NSA_CMP_STRIDE) * NSA_CMP_STRIDE + NSA_CMP_LEN - 1)

    wg_all, wu_all, wd_all = [w.astype(_CD) for w in (expert_w_gate, expert_w_up, expert_w_down)]

    h = x.reshape(T, D)
    for i in range(depth):
        wi = w_in[i]
        w_heads = jnp.concatenate([wi[:, 0:o_kv], wi[:, o_kv + 2 * NSA_KV_W:o_gate], wi[:, o_qkvb:o_merge]],
                                  axis=1).astype(_CD)
        w_rest = jnp.concatenate([wi[:, o_kv:o_kv + 2 * NSA_KV_W], wi[:, o_merge:o_merge + 2 * D],
                                  wi[:, o_gate:o_qkvb], jnp.zeros((D, GATE_PAD - NSA_GATE_W), wi.dtype)],
                                 axis=1).astype(_CD)
        a = _rmsnorm(h, attn_norm[i], _CD)
        zb3 = _head_proj(a, w_heads, cos, sin, S, tm_in).reshape(B, S, HEADS_W)
        zr = _mm_call(_mm_plain_body, [a], [w_rest], [], n_rest, _F32, tm_in, tn, "in_proj_rest")
        zr3 = zr.reshape(B, S, n_rest)
        kvc = _nsa_compress(zr3, nsa_cmp_pe[i], nsa_cmp_w1[i], nsa_cmp_w2[i], ccos, csin)
        o_a = _nsa_attention(zb3, kvc, zr3, col_gate, S).reshape(T, NSA_Q_W)
        o_b = _moba_attention(zb3, S).reshape(T, MOBA_W)
        mixed = _mm_call(_mm_gated_body, [o_a, o_b], [w_up_nsa[i].astype(_CD), w_up_moba[i].astype(_CD)],
                         [(zr, COL_MERGE), (zr, COL_MERGE + D)], D, _CD, tm, tn, "merge_up")
        h = _mm_call(_mm_resid_body, [mixed], [w_out[i].astype(_CD)], [(h, 0)], D, _F32, tm, tn, "out_proj")

        wr = jnp.concatenate([router_group[i], router_expert[i],
                              jnp.zeros((D, LANES - MOE_GROUPS - MOE_EXPERTS), _F32)], axis=1)
        f, eid, wts = _norm_router(h, ffn_norm[i], wr)
        blk_e, nused, gsrc, sdst, buf_w = _group_by_expert(eid[:, :MOE_TOPK], wts[:, :MOE_TOPK])
        y2 = _experts(f, blk_e, nused, gsrc, sdst, buf_w, wg_all, wu_all, wd_all, i)
        h2, n = _add_norm(h, y2, ple_norm[i])
        h = _mm_call(_mm_ple_body, [n, p[i].reshape(T, -1)], [ple_gate[i].astype(_CD), ple_proj[i].astype(_CD)],
                     [(h2, 0)], D, _F32, tm, tn, "ple")
    return _rmsnorm(h, final_norm, _F32).reshape(B, S, D)
```

```python
import functools

import numpy as np
import jax
import jax.numpy as jnp
from jax import lax
from jax.experimental import pallas as pl
from jax.experimental.pallas import tpu as pltpu

HEAD_DIM = 128
ROPE_THETA = 10000.0
NORM_EPS = 1e-6
NEG_INF = -1e30
FORCE_SCORE = 1e9
SCALE = HEAD_DIM ** -0.5

NSA_Q_HEADS = 16
NSA_KV_HEADS = 4
NSA_HPG = NSA_Q_HEADS // NSA_KV_HEADS
NSA_CMP_LEN = 32
NSA_CMP_STRIDE = 16
NSA_SEL_BLOCK = 64
NSA_TOP_N = 16
NSA_WINDOW = 512
NSA_Q_W = NSA_Q_HEADS * HEAD_DIM
NSA_KV_W = NSA_KV_HEADS * HEAD_DIM
NSA_GATE_W = NSA_Q_HEADS * 3

MOBA_HEADS = 16
MOBA_BLOCK = 256
MOBA_TOPK = 3
MOBA_W = MOBA_HEADS * HEAD_DIM

MOE_GROUPS = 4
MOE_EXPERTS_PER_GROUP = 8
MOE_EXPERTS = MOE_GROUPS * MOE_EXPERTS_PER_GROUP
MOE_TOPK = 2

LANES = 128
SEL_PAD = 128
GATE_PAD = 512
VMEM_LIMIT = 48 * 1024 * 1024
VMEM_LIMIT_EXPERTS = 56 * 1024 * 1024

COL_QA = 0
COL_KS = COL_QA + NSA_Q_W
COL_VS = COL_KS + NSA_KV_W
COL_KW = COL_VS + NSA_KV_W
COL_VW = COL_KW + NSA_KV_W
COL_QB = COL_VW + NSA_KV_W
COL_KB = COL_QB + MOBA_W
COL_VB = COL_KB + MOBA_W
HEADS_W = COL_VB + MOBA_W
COL_KC = 0
COL_VC = COL_KC + NSA_KV_W
COL_MERGE = COL_VC + NSA_KV_W

_CD = jnp.bfloat16
_F32 = jnp.float32


def _dot(a, b):
    return jnp.dot(a, b, preferred_element_type=_F32)


def _split(x):
    hi = x.astype(_CD)
    lo = (x - hi.astype(_F32)).astype(_CD)
    return hi, lo


def _t(x):
    return jnp.transpose(x.astype(_F32))


def _params(sem, vmem=VMEM_LIMIT):
    return pltpu.CompilerParams(dimension_semantics=sem, vmem_limit_bytes=vmem)


def _rmsnorm_body(x_ref, g_ref, o_ref):
    x = x_ref[...]
    y = x * lax.rsqrt(jnp.mean(x * x, axis=-1, keepdims=True) + NORM_EPS)
    o_ref[...] = (y * g_ref[...]).astype(o_ref.dtype)


def _rmsnorm(x, g, out_dtype, tm=256):
    T, D = x.shape
    return pl.pallas_call(
        _rmsnorm_body,
        grid=(T // tm,),
        in_specs=[pl.BlockSpec((tm, D), lambda i: (i, 0)),
                  pl.BlockSpec((1, D), lambda i: (0, 0))],
        out_specs=pl.BlockSpec((tm, D), lambda i: (i, 0)),
        out_shape=jax.ShapeDtypeStruct((T, D), out_dtype),
        compiler_params=_params(("parallel",)),
        name="rmsnorm",
    )(x, g.reshape(1, D))


def _mm_call(body, lhs, rhs, extras, n_out, out_dtype, tm, tn, name):
    M = lhs[0].shape[0]
    in_specs = [pl.BlockSpec((tm, a.shape[1]), lambda i, j: (i, 0)) for a in lhs]
    in_specs += [pl.BlockSpec((w.shape[0], tn), lambda i, j: (0, j)) for w in rhs]
    args = list(lhs) + list(rhs)
    for arr, off in extras:
        assert off % tn == 0
        in_specs.append(pl.BlockSpec((tm, tn), functools.partial(lambda i, j, o: (i, j + o), o=off // tn)))
        args.append(arr)
    return pl.pallas_call(
        body,
        grid=(M // tm, n_out // tn),
        in_specs=in_specs,
        out_specs=pl.BlockSpec((tm, tn), lambda i, j: (i, j)),
        out_shape=jax.ShapeDtypeStruct((M, n_out), out_dtype),
        compiler_params=_params(("parallel", "parallel")),
        name=name,
    )(*args)


def _mm_plain_body(a_ref, w_ref, o_ref):
    o_ref[...] = _dot(a_ref[...], w_ref[...]).astype(o_ref.dtype)


def _mm_gated_body(a1_ref, a2_ref, w1_ref, w2_ref, m1_ref, m2_ref, o_ref):
    y1 = _dot(a1_ref[...], w1_ref[...])
    y2 = _dot(a2_ref[...], w2_ref[...])
    o_ref[...] = (jax.nn.sigmoid(m1_ref[...]) * y1 + jax.nn.sigmoid(m2_ref[...]) * y2).astype(o_ref.dtype)


def _mm_resid_body(a_ref, w_ref, r_ref, o_ref):
    o_ref[...] = r_ref[...] + _dot(a_ref[...], w_ref[...])


def _mm_ple_body(n_ref, p_ref, wg_ref, wp_ref, r_ref, o_ref):
    gate = jax.nn.sigmoid(_dot(n_ref[...], wg_ref[...]))
    emb = _dot(p_ref[...].astype(_CD), wp_ref[...])
    o_ref[...] = r_ref[...] + gate * emb


def _rope_tables(pos):
    half = HEAD_DIM // 2
    inv = jnp.power(ROPE_THETA, -jnp.arange(half, dtype=_F32) / half)
    ang = pos.astype(_F32)[:, None] * inv
    c, s = jnp.cos(ang), jnp.sin(ang)
    return jnp.concatenate([c, c], axis=-1), jnp.concatenate([-s, s], axis=-1)


def _rope(x, c, s):
    return x * c + pltpu.roll(x, HEAD_DIM // 2, 1) * s


HEADS_TN = 512
_ROPE_BLOCKS = ((COL_QA // HEADS_TN, COL_KS // HEADS_TN), (COL_KS // HEADS_TN, COL_VS // HEADS_TN),
                (COL_KW // HEADS_TN, COL_VW // HEADS_TN), (COL_QB // HEADS_TN, COL_VB // HEADS_TN))


def _mm_heads_body(a_ref, w_ref, cos_ref, sin_ref, o_ref):
    j = pl.program_id(1)
    y = _dot(a_ref[...], w_ref[...])
    is_rope = functools.reduce(jnp.logical_or, [(j >= a) & (j < b) for a, b in _ROPE_BLOCKS])

    @pl.when(is_rope)
    def _():
        c = cos_ref[...]
        s = sin_ref[...]
        for h in range(HEADS_TN // HEAD_DIM):
            sl = slice(h * HEAD_DIM, (h + 1) * HEAD_DIM)
            o_ref[:, sl] = _rope(y[:, sl], c, s).astype(o_ref.dtype)

    @pl.when(jnp.logical_not(is_rope))
    def _():
        o_ref[...] = y.astype(o_ref.dtype)


def _head_proj(a, w, cos, sin, S, tm):
    T, D = a.shape
    nps = S // tm
    return pl.pallas_call(
        _mm_heads_body,
        grid=(T // tm, HEADS_W // HEADS_TN),
        in_specs=[pl.BlockSpec((tm, D), lambda i, j: (i, 0)),
                  pl.BlockSpec((D, HEADS_TN), lambda i, j: (0, j)),
                  pl.BlockSpec((tm, HEAD_DIM), lambda i, j: (i % nps, 0)),
                  pl.BlockSpec((tm, HEAD_DIM), lambda i, j: (i % nps, 0))],
        out_specs=pl.BlockSpec((tm, HEADS_TN), lambda i, j: (i, j)),
        out_shape=jax.ShapeDtypeStruct((T, HEADS_W), _CD),
        compiler_params=_params(("parallel", "parallel")),
        name="in_proj_heads",
    )(a, w, cos, sin)


def _compress_body(x_ref, pe_ref, w1_ref, w2_ref, cos_ref, sin_ref, o_ref, xpad_ref, *, S, nc):
    ncp = S // NSA_CMP_STRIDE
    xpad_ref[0:S, :] = x_ref[0]
    xpad_ref[S:S + NSA_CMP_LEN, :] = jnp.zeros((NSA_CMP_LEN, HEAD_DIM), _F32)
    acc = jnp.zeros((ncp, HEAD_DIM), _F32)
    for l in range(NSA_CMP_LEN):
        xl = xpad_ref[pl.ds(l, ncp, stride=NSA_CMP_STRIDE), :] + pe_ref[0, l:l + 1, :]
        acc = acc + _dot(xl.astype(_CD), w1_ref[0, l].astype(_CD))
    hid = jax.nn.gelu(acc)
    out = _dot(hid.astype(_CD), w2_ref[0].astype(_CD))
    is_key = pl.program_id(1) == 0
    out = jnp.where(is_key, _rope(out, cos_ref[...], sin_ref[...]), out)
    row = lax.broadcasted_iota(jnp.int32, (ncp, HEAD_DIM), 0)
    o_ref[0, 0, 0] = jnp.where(row < nc, out, 0.0).astype(o_ref.dtype)


def _nsa_compress(zr3, pe, w1, w2, ccos, csin):
    B, S, _ = zr3.shape
    G = NSA_KV_HEADS
    ncp = S // NSA_CMP_STRIDE
    nc = (S - NSA_CMP_LEN) // NSA_CMP_STRIDE + 1
    cb = COL_KC // HEAD_DIM
    return pl.pallas_call(
        functools.partial(_compress_body, S=S, nc=nc),
        grid=(B, 2, G),
        in_specs=[pl.BlockSpec((1, S, HEAD_DIM), lambda b, t, g: (b, 0, cb + t * G + g)),
                  pl.BlockSpec((1, NSA_CMP_LEN, HEAD_DIM), lambda b, t, g: (t, 0, 0)),
                  pl.BlockSpec((1, NSA_CMP_LEN, HEAD_DIM, HEAD_DIM), lambda b, t, g: (t, 0, 0, 0)),
                  pl.BlockSpec((1, HEAD_DIM, HEAD_DIM), lambda b, t, g: (t, 0, 0)),
                  pl.BlockSpec((ncp, HEAD_DIM), lambda b, t, g: (0, 0)),
                  pl.BlockSpec((ncp, HEAD_DIM), lambda b, t, g: (0, 0))],
        out_specs=pl.BlockSpec((1, 1, 1, ncp, HEAD_DIM), lambda b, t, g: (b, t, g, 0, 0)),
        out_shape=jax.ShapeDtypeStruct((B, 2, G, ncp, HEAD_DIM), _CD),
        scratch_shapes=[pltpu.VMEM((S + NSA_CMP_LEN, HEAD_DIM), _F32)],
        compiler_params=_params(("parallel", "parallel", "parallel")),
        name="nsa_compress",
    )(zr3, pe, w1, w2, ccos, csin)


SCALE_LOG2 = SCALE * 1.4426950408889634


def _online_step_t(s2, vt, m, l, acc_ref, allowed=None):
    m_new = jnp.maximum(m, jnp.max(s2, axis=0, keepdims=True))
    alpha = jnp.exp2(m - m_new)
    p = jnp.exp2(s2 - m_new)
    if allowed is not None:
        p = jnp.where(allowed, p, 0.0)
    l = alpha * l + jnp.sum(p, axis=0, keepdims=True)
    acc_ref[...] = alpha * acc_ref[...] + _dot(vt, p.astype(_CD))
    return m_new, l


F32_TINY = 1.1754943508222875e-38


def _next_below(x):
    bits = pltpu.bitcast(x, jnp.int32)
    stepped = pltpu.bitcast(jnp.where(bits > 0, bits - 1, bits + 1), _F32)
    return jnp.where(x == 0.0, -F32_TINY, stepped)


def _rank_rows(val_ref, thr_ref, n_rows):
    thr_ref[...] = _next_below(val_ref[...])

    def step(mm, cnt):
        for u in range(RANK_UNROLL):
            m = mm * RANK_UNROLL + u
            row = val_ref[pl.ds(m, 1), :]
            thr_ref[pl.ds(m, 1), :] = row
            cnt = cnt + jnp.where(row > thr_ref[...], 1.0, 0.0)
        return cnt

    return lax.fori_loop(0, (n_rows + RANK_UNROLL - 1) // RANK_UNROLL, step, jnp.zeros(val_ref.shape, _F32))


def _bias_rows(ref, first, count, rows_each, width):
    return jnp.concatenate([jnp.broadcast_to(ref[pl.ds(first + u, 1), :], (rows_each, width)) for u in range(count)],
                           axis=0)


NSA_TQ = 256
NSA_TK = 1024
RANK_UNROLL = 4


def _nsa_body(q_ref, ks_ref, vs_ref, kw_ref, vw_ref, kc_ref, vc_ref, gate_ref, ovt_ref, o_ref,
              vst_ref, vwt_ref, vct_ref, imp_ref, thr_ref, selb_ref, gt_ref, acc_ref, out_ref, *, S, gate_lane0):
    TQ, TK, H = NSA_TQ, NSA_TK, NSA_HPG
    R = H * TQ
    g = pl.program_id(1)
    qi = pl.program_id(2)
    t0 = qi * TQ
    ncp = S // NSA_CMP_STRIDE

    @pl.when(qi == 0)
    def _():
        def tr(n, c):
            for u in range(TK // LANES):
                r0 = pl.multiple_of(n * TK + u * LANES, LANES)
                cols = slice(u * LANES, (u + 1) * LANES)
                vst_ref[n, :, cols] = _t(vs_ref[0, pl.ds(r0, LANES), :]).astype(_CD)
                vwt_ref[n * (TK // LANES) + u] = _t(vw_ref[0, pl.ds(r0, LANES), :]).astype(_CD)
            return c
        lax.fori_loop(0, S // TK, tr, 0)
        for n in range(ncp // LANES):
            cols = slice(n * LANES, (n + 1) * LANES)
            vct_ref[:, cols] = _t(vc_ref[0, 0, 0, cols, :]).astype(_CD)

    q = q_ref[0]
    qt = jnp.concatenate([_t(q[:, h * HEAD_DIM:(h + 1) * HEAD_DIM]).astype(_CD) for h in range(H)], axis=1)
    pos_l = t0 + (lax.broadcasted_iota(jnp.int32, (1, R), 1) & (TQ - 1))

    gt_ref[...] = jnp.transpose(jax.nn.sigmoid(gate_ref[0]))

    def gate_row(c):
        return jnp.concatenate([gt_ref[pl.ds(gate_lane0 + (g * H + h) * 3 + c, 1), :] for h in range(H)], axis=1)

    cmp_end = lax.broadcasted_iota(jnp.int32, (ncp, 1), 0) * NSA_CMP_STRIDE + (NSA_CMP_LEN - 1)
    s_c = jnp.where(cmp_end <= pos_l, _dot(kc_ref[0, 0, 0], qt) * SCALE_LOG2, NEG_INF)
    p = jnp.exp2(s_c - jnp.max(s_c, axis=0, keepdims=True))
    inv = jnp.where(pos_l >= NSA_CMP_LEN - 1, 1.0 / jnp.sum(p, axis=0, keepdims=True), 0.0)
    p_c = p * inv
    out_ref[...] = gate_row(0) * _dot(vct_ref[...], p_c.astype(_CD))

    p_g = p_c[:, 0:TQ]
    for h in range(1, H):
        p_g = p_g + p_c[:, h * TQ:(h + 1) * TQ]
    p_hi, p_lo = _split(p_g)
    ovt = ovt_ref[...]
    imp = _dot(ovt, p_hi) + _dot(ovt, p_lo)
    j_io = lax.broadcasted_iota(jnp.int32, (SEL_PAD, TQ), 0)
    cur = (t0 + lax.broadcasted_iota(jnp.int32, (SEL_PAD, TQ), 1)) // NSA_SEL_BLOCK
    forced = (j_io == 0) | (j_io == cur) | (j_io == cur - 1)
    imp = jnp.where(j_io <= cur, jnp.where(forced, FORCE_SCORE, imp), NEG_INF)
    imp_ref[...] = imp

    rank = _rank_rows(imp_ref, thr_ref, (t0 + TQ - 1) // NSA_SEL_BLOCK + 1)
    sel_bias = jnp.where(rank < float(NSA_TOP_N), 0.0, NEG_INF)
    selb_ref[...] = jnp.concatenate([sel_bias] * H, axis=1)

    key_io = lax.broadcasted_iota(jnp.int32, (TK, 1), 0)
    init = (jnp.full((1, R), NEG_INF, _F32), jnp.zeros((1, R), _F32))
    spb = TK // NSA_SEL_BLOCK
    n_full = t0 // TK

    def sel_scores(kb):
        k0 = pl.multiple_of(kb * TK, TK)
        s2 = _dot(ks_ref[0, pl.ds(k0, TK), :], qt) * SCALE_LOG2
        return s2 + _bias_rows(selb_ref, kb * spb, spb, NSA_SEL_BLOCK, R)

    acc_ref[...] = jnp.zeros(acc_ref.shape, _F32)
    causal = jnp.where((n_full * TK + key_io) <= pos_l, 0.0, NEG_INF)
    carry = _online_step_t(sel_scores(n_full) + causal, vst_ref[n_full], *init, acc_ref)
    _, l_s = lax.fori_loop(0, n_full, lambda kb, c: _online_step_t(sel_scores(kb), vst_ref[kb], *c, acc_ref), carry)
    out_ref[...] += (gate_row(1) * (1.0 / l_s)) * acc_ref[...]

    wk = NSA_WINDOW + TQ
    w0 = pl.multiple_of(jnp.maximum(t0 - NSA_WINDOW, 0), TQ)
    dist = pos_l - (w0 + lax.broadcasted_iota(jnp.int32, (wk, 1), 0))
    in_win = (dist >= 0) & (dist < NSA_WINDOW)
    s_w = jnp.where(in_win, _dot(kw_ref[0, pl.ds(w0, wk), :], qt) * SCALE_LOG2, NEG_INF)
    p_w = jnp.exp2(s_w - jnp.max(s_w, axis=0, keepdims=True))
    l_w = jnp.sum(p_w, axis=0, keepdims=True)
    p_w = p_w.astype(_CD)
    acc_w = _dot(vwt_ref[w0 // LANES], p_w[0:LANES])
    for u in range(1, wk // LANES):
        acc_w = acc_w + _dot(vwt_ref[w0 // LANES + u], p_w[u * LANES:(u + 1) * LANES])
    out_t = out_ref[...] + (gate_row(2) * (1.0 / l_w)) * acc_w

    for h in range(H):
        o_ref[0, :, h * HEAD_DIM:(h + 1) * HEAD_DIM] = jnp.transpose(out_t[:, h * TQ:(h + 1) * TQ]).astype(o_ref.dtype)


def _overlap_t(S):
    ncp = S // NSA_CMP_STRIDE
    nc = (S - NSA_CMP_LEN) // NSA_CMP_STRIDE + 1
    ns = S // NSA_SEL_BLOCK
    cs = np.arange(nc)[:, None] * NSA_CMP_STRIDE
    ss = np.arange(ns)[None, :] * NSA_SEL_BLOCK
    ov = np.clip(np.minimum(cs + NSA_CMP_LEN, ss + NSA_SEL_BLOCK) - np.maximum(cs, ss), 0, None) / NSA_CMP_LEN
    out = np.zeros((SEL_PAD, ncp), np.float32)
    out[:ns, :nc] = ov.T
    return out


def _nsa_attention(zb3, kvc, zr3, gate_col, S):
    B = zb3.shape[0]
    G, TQ, TK = NSA_KV_HEADS, NSA_TQ, NSA_TK
    assert S // NSA_SEL_BLOCK <= SEL_PAD and S % TK == 0 and TK % TQ == 0 and S >= NSA_WINDOW + TQ
    ncp = S // NSA_CMP_STRIDE
    assert ncp % LANES == 0
    qw = NSA_HPG * HEAD_DIM
    R = NSA_HPG * TQ
    ovt = jnp.asarray(_overlap_t(S)).astype(_CD)
    resident = lambda col: pl.BlockSpec((1, S, HEAD_DIM), lambda b, g, i: (b, 0, col // HEAD_DIM + g))
    return pl.pallas_call(
        functools.partial(_nsa_body, S=S, gate_lane0=gate_col % LANES),
        grid=(B, G, S // TQ),
        in_specs=[pl.BlockSpec((1, TQ, qw), lambda b, g, i: (b, i, g)),
                  resident(COL_KS), resident(COL_VS), resident(COL_KW), resident(COL_VW),
                  pl.BlockSpec((1, 1, 1, ncp, HEAD_DIM), lambda b, g, i: (b, 0, g, 0, 0)),
                  pl.BlockSpec((1, 1, 1, ncp, HEAD_DIM), lambda b, g, i: (b, 1, g, 0, 0)),
                  pl.BlockSpec((1, TQ, LANES), lambda b, g, i: (b, i, gate_col // LANES)),
                  pl.BlockSpec((SEL_PAD, ncp), lambda b, g, i: (0, 0))],
        out_specs=pl.BlockSpec((1, TQ, qw), lambda b, g, i: (b, i, g)),
        out_shape=jax.ShapeDtypeStruct((B, S, NSA_Q_W), _CD),
        scratch_shapes=[pltpu.VMEM((S // TK, HEAD_DIM, TK), _CD),
                        pltpu.VMEM((S // LANES, HEAD_DIM, LANES), _CD),
                        pltpu.VMEM((HEAD_DIM, ncp), _CD),
                        pltpu.VMEM((SEL_PAD, TQ), _F32),
                        pltpu.VMEM((SEL_PAD, TQ), _F32),
                        pltpu.VMEM((SEL_PAD, R), _F32),
                        pltpu.VMEM((LANES, TQ), _F32),
                        pltpu.VMEM((HEAD_DIM, R), _F32),
                        pltpu.VMEM((HEAD_DIM, R), _F32)],
        compiler_params=_params(("parallel", "parallel", "arbitrary")),
        name="nsa_attention",
    )(zb3, zb3, zb3, zb3, zb3, kvc, kvc, zr3, ovt)


MOBA_NB = 4
MOBA_TQ = 1024


def _moba_body(q_ref, k_ref, v_ref, o_ref, kmean_ref, gate_ref, thr_ref, selb_ref, vt_ref, acc_ref, *, S):
    TQ, NB = MOBA_TQ, MOBA_NB
    TK = NB * MOBA_BLOCK
    nf = S // MOBA_BLOCK
    nfp = kmean_ref.shape[0]
    qi = pl.program_id(2)
    t0 = pl.multiple_of(qi * TQ, TQ)

    @pl.when(qi == 0)
    def _():
        kmean_ref[...] = jnp.zeros(kmean_ref.shape, _F32)
        for n in range(nf):
            rows = slice(n * MOBA_BLOCK, (n + 1) * MOBA_BLOCK)
            kmean_ref[n:n + 1, :] = jnp.mean(k_ref[0, rows, :].astype(_F32), axis=0, keepdims=True)
            cols = slice((n % NB) * MOBA_BLOCK, (n % NB + 1) * MOBA_BLOCK)
            vt_ref[n // NB, :, cols] = _t(v_ref[0, rows, :]).astype(_CD)

    qt = _t(q_ref[0]).astype(_CD)
    km_hi, km_lo = _split(kmean_ref[...])
    gate = _dot(km_hi, qt) + _dot(km_lo, qt)
    n_io = lax.broadcasted_iota(jnp.int32, (nfp, TQ), 0)
    qb = (t0 + lax.broadcasted_iota(jnp.int32, (nfp, TQ), 1)) // MOBA_BLOCK
    gate = jnp.where(n_io < qb, gate, NEG_INF)
    gate_ref[...] = gate

    rank = _rank_rows(gate_ref, thr_ref, (t0 + TQ - 1) // MOBA_BLOCK)
    keep = ((rank < float(MOBA_TOPK)) & (n_io < qb)) | (n_io == qb)
    selb_ref[...] = jnp.where(keep, 0.0, NEG_INF)

    pos_l = t0 + lax.broadcasted_iota(jnp.int32, (1, TQ), 1)
    key_io = lax.broadcasted_iota(jnp.int32, (TK, 1), 0)
    init = (jnp.full((1, TQ), NEG_INF, _F32), jnp.zeros((1, TQ), _F32))
    n_full = t0 // TK

    def scores(kb):
        k0 = pl.multiple_of(kb * TK, TK)
        s2 = _dot(k_ref[0, pl.ds(k0, TK), :], qt) * SCALE_LOG2
        return s2 + _bias_rows(selb_ref, kb * NB, NB, MOBA_BLOCK, TQ)

    acc_ref[...] = jnp.zeros(acc_ref.shape, _F32)
    causal = jnp.where((n_full * TK + key_io) <= pos_l, 0.0, NEG_INF)
    carry = _online_step_t(scores(n_full) + causal, vt_ref[n_full], *init, acc_ref)
    _, l = lax.fori_loop(0, n_full, lambda kb, c: _online_step_t(scores(kb), vt_ref[kb], *c, acc_ref), carry)
    o_ref[0] = jnp.transpose(acc_ref[...] * (1.0 / l)).astype(o_ref.dtype)


def _moba_attention(zb3, S):
    B = zb3.shape[0]
    H, TQ = MOBA_HEADS, MOBA_TQ
    assert S % (MOBA_NB * MOBA_BLOCK) == 0 and (MOBA_NB * MOBA_BLOCK) % TQ == 0 and TQ % MOBA_BLOCK == 0
    nf = S // MOBA_BLOCK
    nfp = -(-nf // 8) * 8
    assert nfp % RANK_UNROLL == 0
    tile = lambda col: pl.BlockSpec((1, TQ, HEAD_DIM), lambda b, h, i: (b, i, col // HEAD_DIM + h))
    resident = lambda col: pl.BlockSpec((1, S, HEAD_DIM), lambda b, h, i: (b, 0, col // HEAD_DIM + h))
    return pl.pallas_call(
        functools.partial(_moba_body, S=S),
        grid=(B, H, S // TQ),
        in_specs=[tile(COL_QB), resident(COL_KB), resident(COL_VB)],
        out_specs=pl.BlockSpec((1, TQ, HEAD_DIM), lambda b, h, i: (b, i, h)),
        out_shape=jax.ShapeDtypeStruct((B, S, MOBA_W), _CD),
        scratch_shapes=[pltpu.VMEM((nfp, HEAD_DIM), _F32),
                        pltpu.VMEM((nfp, TQ), _F32),
                        pltpu.VMEM((nfp, TQ), _F32),
                        pltpu.VMEM((nfp, TQ), _F32),
                        pltpu.VMEM((nf // MOBA_NB, HEAD_DIM, MOBA_NB * MOBA_BLOCK), _CD),
                        pltpu.VMEM((HEAD_DIM, TQ), _F32)],
        compiler_params=_params(("parallel", "parallel", "arbitrary")),
        name="moba_attention",
    )(zb3, zb3, zb3)


def _router_body(h_ref, g_ref, wr_ref, f_ref, eid_ref, wts_ref):
    x = h_ref[...]
    y = x * lax.rsqrt(jnp.mean(x * x, axis=-1, keepdims=True) + NORM_EPS)
    f = y * g_ref[...]
    f_ref[...] = f
    f_hi, f_lo = _split(f)
    w_hi, w_lo = _split(wr_ref[...])
    logits = _dot(f_hi, w_hi) + (_dot(f_hi, w_lo) + _dot(f_lo, w_hi))
    lane = lax.broadcasted_iota(jnp.int32, logits.shape, 1)
    big = jnp.int32(2 * LANES)
    in_g = lane < MOE_GROUPS
    lg = jnp.where(in_g, logits, -jnp.inf)
    mg = jnp.max(lg, axis=1, keepdims=True)
    pg_top = 1.0 / jnp.sum(jnp.where(in_g, jnp.exp(logits - mg), 0.0), axis=1, keepdims=True)
    grp = jnp.min(jnp.where(lg == mg, lane, big), axis=1, keepdims=True)
    eidx = lane - MOE_GROUPS
    in_e = (eidx >= 0) & (eidx < MOE_EXPERTS) & ((eidx // MOE_EXPERTS_PER_GROUP) == grp)
    le = jnp.where(in_e, logits, -jnp.inf)
    m1 = jnp.max(le, axis=1, keepdims=True)
    i1 = jnp.min(jnp.where(le == m1, lane, big), axis=1, keepdims=True)
    le2 = jnp.where(lane == i1, -jnp.inf, le)
    m2 = jnp.max(le2, axis=1, keepdims=True)
    i2 = jnp.min(jnp.where(le2 == m2, lane, big), axis=1, keepdims=True)
    e2 = jnp.exp(m2 - m1)
    w1 = pg_top / (1.0 + e2)
    w2 = pg_top * e2 / (1.0 + e2)
    eid_ref[...] = jnp.where(lane == 0, i1 - MOE_GROUPS, jnp.where(lane == 1, i2 - MOE_GROUPS, 0))
    wts_ref[...] = jnp.where(lane == 0, w1, jnp.where(lane == 1, w2, 0.0))


def _norm_router(h, g, wr, tm=256):
    T, D = h.shape
    row = lambda w: pl.BlockSpec((tm, w), lambda i: (i, 0))
    return pl.pallas_call(
        _router_body,
        grid=(T // tm,),
        in_specs=[row(D), pl.BlockSpec((1, D), lambda i: (0, 0)), pl.BlockSpec((D, LANES), lambda i: (0, 0))],
        out_specs=[row(D), row(LANES), row(LANES)],
        out_shape=[jax.ShapeDtypeStruct((T, D), _F32), jax.ShapeDtypeStruct((T, LANES), jnp.int32),
                   jax.ShapeDtypeStruct((T, LANES), _F32)],
        compiler_params=_params(("parallel",)),
        name="ffn_norm_router",
    )(h, g.reshape(1, D), wr)


MOE_ROWS = 256


def _expert_body(blk_e_ref, nused_ref, gsrc_ref, sdst_ref, f_hbm, w_ref, wg_ref, wu_ref, wd_ref, y_hbm,
                 xbuf, obuf, sem_in, sem_out):
    i = pl.program_id(0)
    nused = nused_ref[0]
    slot = i % 2
    other = 1 - slot

    def gather(blk, s, r):
        return pltpu.make_async_copy(f_hbm.at[pl.ds(gsrc_ref[blk * MOE_ROWS + r], 1)], xbuf.at[s, pl.ds(r, 1)],
                                     sem_in.at[s])

    def scatter(blk1, s, r):
        return pltpu.make_async_copy(obuf.at[s, pl.ds(r, 1)], y_hbm.at[pl.ds(sdst_ref[blk1 * MOE_ROWS + r], 1)],
                                     sem_out.at[s])

    def wait_gather(s):
        pltpu.make_async_copy(f_hbm.at[pl.ds(0, MOE_ROWS)], xbuf.at[s], sem_in.at[s]).wait()

    def wait_scatter(s):
        pltpu.make_async_copy(obuf.at[s], y_hbm.at[pl.ds(0, MOE_ROWS)], sem_out.at[s]).wait()

    @pl.when(i == 0)
    def _():
        obuf[...] = jnp.zeros(obuf.shape, _F32)
        spare0 = pltpu.make_async_copy(obuf.at[0], y_hbm.at[pl.ds(y_hbm.shape[0] - 2 * MOE_ROWS, MOE_ROWS)],
                                       sem_out.at[0])
        spare0.start()
        spare0.wait()

        def start(r, c):
            gather(0, 0, r).start()
            return c
        lax.fori_loop(0, MOE_ROWS, start, 0)

    @pl.when(i <= nused)
    def _():
        wait_gather(slot)

        @pl.when(i >= 1)
        def _():
            wait_scatter(slot)

        for r in range(MOE_ROWS):
            gather(i + 1, other, r).start()
            scatter(i, other, r).start()

        x = xbuf[slot].astype(_CD)
        hb = (jax.nn.silu(_dot(x, wg_ref[0, 0])) * _dot(x, wu_ref[0, 0])).astype(_CD)
        obuf[slot] = _dot(hb, wd_ref[0, 0]) * w_ref[...]

        @pl.when(i == nused)
        def _():
            wait_scatter(other)
            wait_gather(other)


def _experts(f, blk_e, nused, gsrc, sdst, buf_w, wg, wu, wd, layer):
    T, D = f.shape
    FF = wg.shape[-1]
    n_blocks = blk_e.shape[0]
    once = pl.Buffered(1)
    expert = lambda i, be, nu, gs, sd: (layer, be[i], 0, 0)
    grid_spec = pltpu.PrefetchScalarGridSpec(
        num_scalar_prefetch=4,
        grid=(n_blocks,),
        in_specs=[pl.BlockSpec(memory_space=pl.ANY),
                  pl.BlockSpec((MOE_ROWS, 1), lambda i, be, nu, gs, sd: (i, 0)),
                  pl.BlockSpec((1, 1, D, FF), expert, pipeline_mode=once),
                  pl.BlockSpec((1, 1, D, FF), expert, pipeline_mode=once),
                  pl.BlockSpec((1, 1, FF, D), expert, pipeline_mode=once)],
        out_specs=pl.BlockSpec(memory_space=pl.ANY),
        scratch_shapes=[pltpu.VMEM((2, MOE_ROWS, D), _F32), pltpu.VMEM((2, MOE_ROWS, D), _F32),
                        pltpu.SemaphoreType.DMA((2,)), pltpu.SemaphoreType.DMA((2,))],
    )
    return pl.pallas_call(
        _expert_body,
        grid_spec=grid_spec,
        out_shape=jax.ShapeDtypeStruct((MOE_TOPK * T + 2 * MOE_ROWS, D), _F32),
        compiler_params=_params(("arbitrary",), VMEM_LIMIT_EXPERTS),
        name="moe_experts",
    )(blk_e, nused, gsrc, sdst, f, buf_w, wg, wu, wd)


def _group_by_expert(eid, wts):
    T = eid.shape[0]
    n_assign = T * MOE_TOPK
    n_blocks = -(-n_assign // MOE_ROWS) + MOE_EXPERTS
    e_flat = eid.reshape(-1)
    order = jnp.argsort(e_flat, stable=True).astype(jnp.int32)
    counts = jnp.bincount(e_flat, length=MOE_EXPERTS).astype(jnp.int32)
    starts = jnp.cumsum(counts) - counts
    pcounts = (counts + MOE_ROWS - 1) // MOE_ROWS * MOE_ROWS
    pends = jnp.cumsum(pcounts)
    pstarts = pends - pcounts
    blk_start = jnp.arange(n_blocks, dtype=jnp.int32) * MOE_ROWS
    blk_e = jnp.minimum(jnp.sum(pends[None, :] <= blk_start[:, None], axis=1), MOE_EXPERTS - 1).astype(jnp.int32)
    slot = jnp.arange(n_blocks * MOE_ROWS, dtype=jnp.int32)
    per_slot = lambda table: jnp.repeat(table[blk_e], MOE_ROWS)
    off = slot - per_slot(pstarts)
    valid = (off >= 0) & (off < per_slot(counts))
    src = order[jnp.clip(per_slot(starts) + off, 0, n_assign - 1)]
    buf_w = jnp.where(valid, wts.reshape(-1)[src], 0.0)
    spare = n_assign + ((slot // MOE_ROWS) % 2) * MOE_ROWS + slot % MOE_ROWS
    gsrc = jnp.where(valid, src // MOE_TOPK, 0)
    sdst = jnp.where(valid, (src % MOE_TOPK) * T + src // MOE_TOPK, spare)
    gsrc = jnp.concatenate([gsrc, jnp.zeros((MOE_ROWS,), jnp.int32)]).astype(jnp.int32)
    sdst = jnp.concatenate([n_assign + MOE_ROWS + jnp.arange(MOE_ROWS, dtype=jnp.int32), sdst]).astype(jnp.int32)
    nused = (pends[-1:] // MOE_ROWS).astype(jnp.int32)
    return blk_e, nused, gsrc, sdst, buf_w.reshape(-1, 1)


def _addnorm_body(h_ref, y0_ref, y1_ref, g_ref, h2_ref, n_ref):
    h2 = h_ref[...] + (y0_ref[...] + y1_ref[...])
    h2_ref[...] = h2
    y = h2 * lax.rsqrt(jnp.mean(h2 * h2, axis=-1, keepdims=True) + NORM_EPS)
    n_ref[...] = (y * g_ref[...]).astype(n_ref.dtype)


def _add_norm(h, y2, g, tm=256):
    T, D = h.shape
    nt = T // tm
    return pl.pallas_call(
        _addnorm_body,
        grid=(nt,),
        in_specs=[pl.BlockSpec((tm, D), lambda i: (i, 0)),
                  pl.BlockSpec((tm, D), lambda i: (i, 0)),
                  pl.BlockSpec((tm, D), lambda i: (nt + i, 0)),
                  pl.BlockSpec((1, D), lambda i: (0, 0))],
        out_specs=[pl.BlockSpec((tm, D), lambda i: (i, 0)), pl.BlockSpec((tm, D), lambda i: (i, 0))],
        out_shape=[jax.ShapeDtypeStruct((T, D), _F32), jax.ShapeDtypeStruct((T, D), _CD)],
        compiler_params=_params(("parallel",)),
        name="moe_add_norm",
    )(h, y2, y2, g.reshape(1, D))


def kernel(x, p, attn_norm, w_in, nsa_cmp_pe, nsa_cmp_w1, nsa_cmp_w2, w_up_nsa, w_up_moba, w_out, ffn_norm, router_group, router_expert, expert_w_gate, expert_w_up, expert_w_down, ple_norm, ple_gate, ple_proj, final_norm):
    B, S, D = x.shape
    depth = w_in.shape[0]
    T = B * S
    tm = min(1024, T)
    tm_in = min(1024, S)
    tn = min(512, D)
    assert D % tn == 0 and T % tm == 0 and S % tm_in == 0

    o_kv = NSA_Q_W
    o_gate = o_kv + 6 * NSA_KV_W
    o_qkvb = o_gate + NSA_GATE_W
    o_merge = o_qkvb + 3 * MOBA_W
    col_gate = COL_MERGE + 2 * D
    n_rest = col_gate + GATE_PAD

    cos, sin = _rope_tables(jnp.arange(S))
    ccos, csin = _rope_tables(jnp.arange(S // NSA_CMP_STRIDE) * NSA_CMP_STRIDE + NSA_CMP_LEN - 1)

    wg_all, wu_all, wd_all = [w.astype(_CD) for w in (expert_w_gate, expert_w_up, expert_w_down)]

    h = x.reshape(T, D)
    for i in range(depth):
        wi = w_in[i]
        w_heads = jnp.concatenate([wi[:, 0:o_kv], wi[:, o_kv + 2 * NSA_KV_W:o_gate], wi[:, o_qkvb:o_merge]],
                                  axis=1).astype(_CD)
        w_rest = jnp.concatenate([wi[:, o_kv:o_kv + 2 * NSA_KV_W], wi[:, o_merge:o_merge + 2 * D],
                                  wi[:, o_gate:o_qkvb], jnp.zeros((D, GATE_PAD - NSA_GATE_W), wi.dtype)],
                                 axis=1).astype(_CD)
        a = _rmsnorm(h, attn_norm[i], _CD)
        zb3 = _head_proj(a, w_heads, cos, sin, S, tm_in).reshape(B, S, HEADS_W)
        zr = _mm_call(_mm_plain_body, [a], [w_rest], [], n_rest, _F32, tm_in, tn, "in_proj_rest")
        zr3 = zr.reshape(B, S, n_rest)
        kvc = _nsa_compress(zr3, nsa_cmp_pe[i], nsa_cmp_w1[i], nsa_cmp_w2[i], ccos, csin)
        o_a = _nsa_attention(zb3, kvc, zr3, col_gate, S).reshape(T, NSA_Q_W)
        o_b = _moba_attention(zb3, S).reshape(T, MOBA_W)
        mixed = _mm_call(_mm_gated_body, [o_a, o_b], [w_up_nsa[i].astype(_CD), w_up_moba[i].astype(_CD)],
                         [(zr, COL_MERGE), (zr, COL_MERGE + D)], D, _CD, tm, tn, "merge_up")
        h = _mm_call(_mm_resid_body, [mixed], [w_out[i].astype(_CD)], [(h, 0)], D, _F32, tm, tn, "out_proj")

        wr = jnp.concatenate([router_group[i], router_expert[i],
                              jnp.zeros((D, LANES - MOE_GROUPS - MOE_EXPERTS), _F32)], axis=1)
        f, eid, wts = _norm_router(h, ffn_norm[i], wr)
        blk_e, nused, gsrc, sdst, buf_w = _group_by_expert(eid[:, :MOE_TOPK], wts[:, :MOE_TOPK])
        y2 = _experts(f, blk_e, nused, gsrc, sdst, buf_w, wg_all, wu_all, wd_all, i)
        h2, n = _add_norm(h, y2, ple_norm[i])
        h = _mm_call(_mm_ple_body, [n, p[i].reshape(T, -1)], [ple_gate[i].astype(_CD), ple_proj[i].astype(_CD)],
                     [(h2, 0)], D, _F32, tm, tn, "ple")
    return _rmsnorm(h, final_norm, _F32).reshape(B, S, D)
```
